```python
import jax, jax.numpy as jnp
from jax import lax
import numpy as np

D_MODEL = 1024
BATCH = 8
SEQ = 2048
DEPTH = 1

CONV_CH = 512
N_CONV_GROUPS = 8
ATTN_HEADS = 8
HEAD_DIM = 64
ATTN_WIDTH = ATTN_HEADS * HEAD_DIM
MIX_WIDTH = CONV_CH + ATTN_WIDTH
IN_WIDTH = 2 * CONV_CH + 3 * ATTN_WIDTH
CONV_KERNEL = 31
MOBA_BLOCK = 256
MOBA_TOPK = 3
QUERY_CHUNK = 32
ROPE_THETA = 500000.0
ROPE_DIM = HEAD_DIM // 4
D_FF = -(-8 * D_MODEL // (3 * 256)) * 256
PLE_DIM = 256
EPS = 1e-6

kernel_name = "hymba_conformer_moba_hybrid"


def rmsnorm(x, g):
    xf = x.astype(jnp.float32)
    y = xf * lax.rsqrt(jnp.mean(xf * xf, axis=-1, keepdims=True) + EPS)
    return (y * g.astype(jnp.float32)).astype(x.dtype)


def layernorm(x, g, b):
    xf = x.astype(jnp.float32)
    mu = jnp.mean(xf, axis=-1, keepdims=True)
    var = jnp.mean(jnp.square(xf - mu), axis=-1, keepdims=True)
    y = (xf - mu) * lax.rsqrt(var + EPS)
    return (y * g.astype(jnp.float32) + b.astype(jnp.float32)).astype(x.dtype)


def rope_tables(positions, dtype):
    inv_freq = ROPE_THETA ** (-jnp.arange(0, ROPE_DIM, 2, dtype=jnp.float32) / ROPE_DIM)
    ang = positions.astype(jnp.float32)[..., None] * inv_freq
    return jnp.cos(ang)[:, None].astype(dtype), jnp.sin(ang)[:, None].astype(dtype)


def apply_partial_rope(x, cos, sin):
    half = ROPE_DIM // 2
    x1, x2, rest = x[..., :half], x[..., half:ROPE_DIM], x[..., ROPE_DIM:]
    return jnp.concatenate([x1 * cos - x2 * sin, x2 * cos + x1 * sin, rest], axis=-1)


def conv_mixer(a, g, w_dw, b_dw, ln_g, ln_b):
    u = a * jax.nn.sigmoid(g)
    c = u.shape[-1]
    u = lax.conv_general_dilated(
        u, w_dw.astype(u.dtype)[:, None, :], window_strides=(1,),
        padding=[(CONV_KERNEL - 1, 0)], dimension_numbers=("NWC", "WIO", "NWC"),
        feature_group_count=c) + b_dw.astype(u.dtype)
    return jax.nn.silu(layernorm(u, ln_g, ln_b))


def moba_attention(q, k, v):
    B, H, S, Dh = q.shape
    nb = -(-S // MOBA_BLOCK)
    s_pad = nb * MOBA_BLOCK
    topk = min(MOBA_TOPK, nb)
    pad = ((0, 0), (0, 0), (0, s_pad - S), (0, 0))
    kp = jnp.pad(k, pad)
    vp = jnp.pad(v, pad)
    k_blk = kp.reshape(B, H, nb, MOBA_BLOCK, Dh)
    v_blk = vp.reshape(B, H, nb, MOBA_BLOCK, Dh)
    k_mean = jnp.mean(k_blk.astype(jnp.float32), axis=3)
    scale = Dh ** -0.5
    n_chunks = S // QUERY_CHUNK
    q_chunks = q.reshape(B, H, n_chunks, QUERY_CHUNK, Dh).transpose(2, 0, 1, 3, 4)
    b_idx = jnp.arange(B)[:, None, None, None]
    h_idx = jnp.arange(H)[None, :, None, None]
    neg = jnp.finfo(jnp.float32).min

    def one_chunk(args):
        qc, c = args
        q_start = c * QUERY_CHUNK
        own = q_start // MOBA_BLOCK
        q_pos = q_start + jnp.arange(QUERY_CHUNK)
        gate = jnp.einsum('bhqd,bhnd->bhqn', qc.astype(jnp.float32), k_mean)
        gate = jnp.where(jnp.arange(nb) < own, gate, neg)
        _, sel = lax.top_k(gate, topk)
        slot_valid = jnp.arange(topk) < own
        k_sel = k_blk[b_idx, h_idx, sel]
        v_sel = v_blk[b_idx, h_idx, sel]
        s_sel = jnp.einsum('bhqd,bhqtkd->bhqtk', qc, k_sel).astype(jnp.float32) * scale
        s_sel = jnp.where(slot_valid[:, None], s_sel, neg)
        s_sel = s_sel.reshape(B, H, QUERY_CHUNK, topk * MOBA_BLOCK)
        k_own = lax.dynamic_slice_in_dim(kp, own * MOBA_BLOCK, MOBA_BLOCK, axis=2)
        v_own = lax.dynamic_slice_in_dim(vp, own * MOBA_BLOCK, MOBA_BLOCK, axis=2)
        s_own = jnp.einsum('bhqd,bhkd->bhqk', qc, k_own).astype(jnp.float32) * scale
        k_pos = own * MOBA_BLOCK + jnp.arange(MOBA_BLOCK)
        s_own = jnp.where(k_pos[None, :] <= q_pos[:, None], s_own, neg)
        probs = jax.nn.softmax(jnp.concatenate([s_sel, s_own], axis=-1), axis=-1)
        p_sel = probs[..., :topk * MOBA_BLOCK].reshape(B, H, QUERY_CHUNK, topk, MOBA_BLOCK)
        p_own = probs[..., topk * MOBA_BLOCK:]
        return (jnp.einsum('bhqtk,bhqtkd->bhqd', p_sel.astype(v.dtype), v_sel)
                + jnp.einsum('bhqk,bhkd->bhqd', p_own.astype(v.dtype), v_own))

    out = lax.map(one_chunk, (q_chunks, jnp.arange(n_chunks)))
    return out.transpose(1, 2, 0, 3, 4).reshape(B, H, S, Dh)


def setup_inputs(seed: int = 0) -> dict:
    key = jax.random.key(seed)
    ks = jax.random.split(key, 20)
    f32 = jnp.float32
    nrm = lambda k, shape, s: jax.random.normal(k, shape, f32) * s
    gain = lambda k, shape: 1.0 + 0.05 * jax.random.normal(k, shape, f32)
    return {
        "x": jax.random.normal(ks[0], (BATCH, SEQ, D_MODEL), f32),
        "p": jax.random.normal(ks[1], (DEPTH, BATCH, SEQ, PLE_DIM), f32),
        "positions": jnp.broadcast_to(jnp.arange(SEQ, dtype=jnp.int32), (BATCH, SEQ)),
        "norm_mix_g": gain(ks[2], (DEPTH, D_MODEL)),
        "w_in": nrm(ks[3], (DEPTH, D_MODEL, IN_WIDTH), D_MODEL ** -0.5),
        "conv_w": nrm(ks[4], (DEPTH, CONV_KERNEL, CONV_CH), CONV_KERNEL ** -0.5),
        "conv_b": nrm(ks[5], (DEPTH, CONV_CH), 0.02),
        "conv_ln_g": gain(ks[6], (DEPTH, CONV_CH)),
        "conv_ln_b": nrm(ks[7], (DEPTH, CONV_CH), 0.02),
        "w_out": nrm(ks[8], (DEPTH, MIX_WIDTH, D_MODEL), MIX_WIDTH ** -0.5),
        "norm_ffn_g": gain(ks[9], (DEPTH, D_MODEL)),
        "w_ffn_up": nrm(ks[10], (DEPTH, D_MODEL, 2 * D_FF), D_MODEL ** -0.5),
        "w_ffn_down": nrm(ks[11], (DEPTH, D_FF, D_MODEL), D_FF ** -0.5),
        "norm_ple_g": gain(ks[12], (DEPTH, D_MODEL)),
        "w_ple_gate": nrm(ks[13], (DEPTH, D_MODEL, D_MODEL), D_MODEL ** -0.5),
        "w_ple_proj": nrm(ks[14], (DEPTH, PLE_DIM, D_MODEL), PLE_DIM ** -0.5),
        "final_norm_g": gain(ks[15], (D_MODEL,)),
    }


def reference(x, p, positions, norm_mix_g, w_in, conv_w, conv_b, conv_ln_g, conv_ln_b,
              w_out, norm_ffn_g, w_ffn_up, w_ffn_down, norm_ple_g, w_ple_gate,
              w_ple_proj, final_norm_g):
    B, S, _ = x.shape
    cos, sin = rope_tables(positions, x.dtype)
    splits = [CONV_CH, 2 * CONV_CH, 2 * CONV_CH + ATTN_WIDTH, 2 * CONV_CH + 2 * ATTN_WIDTH]
    to_heads = lambda t: t.reshape(B, S, ATTN_HEADS, HEAD_DIM).transpose(0, 2, 1, 3)
    h = x
    for i in range(DEPTH):
        hn = rmsnorm(h, norm_mix_g[i])
        z = hn @ w_in[i]
        a, g, q, k, v = jnp.split(z, splits, axis=-1)
        conv_out = conv_mixer(a, g, conv_w[i], conv_b[i], conv_ln_g[i], conv_ln_b[i])
        q = apply_partial_rope(to_heads(q), cos, sin)
        k = apply_partial_rope(to_heads(k), cos, sin)
        attn = moba_attention(q, k, to_heads(v))
        attn = attn.transpose(0, 2, 1, 3).reshape(B, S, ATTN_WIDTH)
        h = h + jnp.concatenate([conv_out, attn], axis=-1) @ w_out[i]
        hn = rmsnorm(h, norm_ffn_g[i])
        gt, up = jnp.split(hn @ w_ffn_up[i], 2, axis=-1)
        h = h + (jax.nn.silu(gt) * up) @ w_ffn_down[i]
        gate = jax.nn.sigmoid(rmsnorm(h, norm_ple_g[i]) @ w_ple_gate[i])
        h = h + gate * (p[i].astype(h.dtype) @ w_ple_proj[i])
    return rmsnorm(h, final_norm_g)
```

```python
import functools

import jax
import jax.numpy as jnp
from jax import lax
from jax.experimental import pallas as pl
from jax.experimental.pallas import tpu as pltpu

D_MODEL = 1024
CONV_CH = 512
ATTN_HEADS = 8
HEAD_DIM = 64
ATTN_WIDTH = ATTN_HEADS * HEAD_DIM
IN_WIDTH = 2 * CONV_CH + 3 * ATTN_WIDTH
CONV_KERNEL = 31
MOBA_BLOCK = 256
MOBA_TOPK = 3
ROPE_THETA = 500000.0
ROPE_DIM = HEAD_DIM // 4
ROPE_HALF = ROPE_DIM // 2
D_FF = -(-8 * D_MODEL // (3 * 256)) * 256
PLE_DIM = 256
EPS = 1e-6

LANES = 128
SUBLANES = 8
HEADS_PER_STEP = LANES // HEAD_DIM
CONV_PAD = 32
CONV_ROWS = 64
IN_ROWS = 512
TAIL_ROWS = 256
VMEM_LIMIT = 56 * 1024 * 1024

F32 = jnp.float32
BF16 = jnp.bfloat16
NEG = float(jnp.finfo(jnp.float32).min)


def _rmsnorm(x, g):
    return x * lax.rsqrt(jnp.mean(x * x, axis=-1, keepdims=True) + EPS) * g


def _dot(a, b):
    return jnp.dot(a, b, preferred_element_type=F32)


def _dot_nt(a, b):
    return lax.dot_general(a, b, (((1,), (1,)), ((), ())), preferred_element_type=F32)


def _in_proj_kernel(x_ref, pos_ref, freq_ref, g_ref, w_ref, u_ref, q_ref, k_ref, v_ref):
    hn = _rmsnorm(x_ref[...], g_ref[...]).astype(BF16)
    z = _dot(hn, w_ref[...])
    u_ref[...] = z[:, :CONV_CH] * jax.nn.sigmoid(z[:, CONV_CH:2 * CONV_CH])

    ang = pos_ref[...].astype(F32) * freq_ref[...]
    cos = jnp.cos(ang)
    sin = jnp.sin(ang)
    d = lax.broadcasted_iota(jnp.int32, (1, LANES), 1) & (HEAD_DIM - 1)
    sin_from_hi = jnp.where(d < ROPE_HALF, -sin, 0.0)
    sin_from_lo = jnp.where(d >= ROPE_HALF, sin, 0.0)

    def rope(t):
        return (t * cos
                + pltpu.roll(t, ROPE_HALF, 1) * sin_from_lo
                + pltpu.roll(t, LANES - ROPE_HALF, 1) * sin_from_hi)

    q0 = 2 * CONV_CH
    k0 = q0 + ATTN_WIDTH
    v0 = k0 + ATTN_WIDTH
    scale = HEAD_DIM ** -0.5
    for c in range(ATTN_WIDTH // LANES):
        sl = slice(c * LANES, (c + 1) * LANES)
        q_ref[:, sl] = (rope(z[:, q0 + c * LANES:q0 + (c + 1) * LANES]) * scale).astype(BF16)
        k_ref[:, sl] = rope(z[:, k0 + c * LANES:k0 + (c + 1) * LANES]).astype(BF16)
    v_ref[...] = z[:, v0:v0 + ATTN_WIDTH].astype(BF16)


def _in_proj(x2, pos2, freq, g, w):
    t = x2.shape[0]
    rows = IN_ROWS
    row_spec = lambda width: pl.BlockSpec((rows, width), lambda i: (i, 0))
    full = lambda a: pl.BlockSpec(a.shape, lambda i: (0, 0))
    return pl.pallas_call(
        _in_proj_kernel,
        grid=(t // rows,),
        in_specs=[row_spec(D_MODEL), row_spec(1), full(freq), full(g), full(w)],
        out_specs=[row_spec(CONV_CH), row_spec(ATTN_WIDTH), row_spec(ATTN_WIDTH), row_spec(ATTN_WIDTH)],
        out_shape=[jax.ShapeDtypeStruct((t, CONV_CH), F32),
                   jax.ShapeDtypeStruct((t, ATTN_WIDTH), BF16),
                   jax.ShapeDtypeStruct((t, ATTN_WIDTH), BF16),
                   jax.ShapeDtypeStruct((t, ATTN_WIDTH), BF16)],
        compiler_params=pltpu.CompilerParams(
            dimension_semantics=("arbitrary",), vmem_limit_bytes=VMEM_LIMIT),
        name="in_proj",
    )(x2, pos2, freq, g, w)


def _conv_kernel(u_ref, w_ref, b_ref, lg_ref, lb_ref, o_ref, pad_ref):
    seq = u_ref.shape[0]
    pad_ref[0:CONV_PAD, :] = jnp.zeros((CONV_PAD, CONV_CH), F32)
    pad_ref[CONV_PAD:, :] = u_ref[...]
    first = CONV_PAD - (CONV_KERNEL - 1)

    def step(c, carry):
        t0 = pl.multiple_of(c * CONV_ROWS, CONV_ROWS)
        win = pad_ref[pl.ds(t0, CONV_ROWS + CONV_PAD), :]
        acc = jnp.broadcast_to(b_ref[...], (CONV_ROWS, CONV_CH))
        for phase in range(SUBLANES):
            shifted = win if phase == 0 else win[phase:phase + CONV_ROWS + CONV_PAD - SUBLANES]
            for base in range(0, CONV_PAD + 1, SUBLANES):
                j = base + phase - first
                if 0 <= j < CONV_KERNEL:
                    acc = acc + w_ref[j:j + 1, :] * shifted[base:base + CONV_ROWS]
        mu = jnp.mean(acc, axis=-1, keepdims=True)
        dev = acc - mu
        var = jnp.mean(dev * dev, axis=-1, keepdims=True)
        y = dev * lax.rsqrt(var + EPS) * lg_ref[...] + lb_ref[...]
        o_ref[pl.ds(t0, CONV_ROWS), :] = (y * jax.nn.sigmoid(y)).astype(BF16)
        return carry

    lax.fori_loop(0, seq // CONV_ROWS, step, 0)


def _conv(u3, w, b, lg, lb):
    bsz, seq, _ = u3.shape
    full = lambda a: pl.BlockSpec(a.shape, lambda i: (0, 0))
    return pl.pallas_call(
        _conv_kernel,
        grid=(bsz,),
        in_specs=[pl.BlockSpec((None, seq, CONV_CH), lambda i: (i, 0, 0)),
                  full(w), full(b), full(lg), full(lb)],
        out_specs=pl.BlockSpec((None, seq, CONV_CH), lambda i: (i, 0, 0)),
        out_shape=jax.ShapeDtypeStruct((bsz, seq, CONV_CH), BF16),
        scratch_shapes=[pltpu.VMEM((seq + CONV_PAD, CONV_CH), F32)],
        compiler_params=pltpu.CompilerParams(
            dimension_semantics=("arbitrary",), vmem_limit_bytes=VMEM_LIMIT),
        name="conv",
    )(u3, w, b, lg, lb)


def _attn_kernel(q_ref, k_ref, v_ref, o_ref):
    seq = q_ref.shape[0]
    nb = seq // MOBA_BLOCK
    q = q_ref[...]
    k = k_ref[...]
    v = v_ref[...]
    lane = lax.broadcasted_iota(jnp.int32, (1, LANES), 1)
    kmean = jnp.mean(k.astype(F32).reshape(nb, MOBA_BLOCK, LANES), axis=1)
    row = lax.broadcasted_iota(jnp.int32, (MOBA_BLOCK, MOBA_BLOCK), 0)
    col = lax.broadcasted_iota(jnp.int32, (MOBA_BLOCK, MOBA_BLOCK), 1)
    causal = col <= row

    outs = [None] * nb
    for h in range(HEADS_PER_STEP):
        in_head = (lane >= h * HEAD_DIM) & (lane < (h + 1) * HEAD_DIM)
        qh = jnp.where(in_head, q, jnp.zeros_like(q))
        vh = jnp.where(in_head, v, jnp.zeros_like(v))
        kmh = jnp.where(in_head, kmean, 0.0).astype(BF16)
        gate = _dot_nt(qh, kmh)
        for i in range(nb):
            qi = qh[i * MOBA_BLOCK:(i + 1) * MOBA_BLOCK]
            s = _dot_nt(qi, k[:(i + 1) * MOBA_BLOCK])
            g = [gate[i * MOBA_BLOCK:(i + 1) * MOBA_BLOCK, n:n + 1] for n in range(i)]
            pieces = []
            for j in range(i + 1):
                sj = s[:, j * MOBA_BLOCK:(j + 1) * MOBA_BLOCK]
                if j == i:
                    sj = jnp.where(causal, sj, NEG)
                elif i > MOBA_TOPK:
                    rank = jnp.zeros((MOBA_BLOCK, 1), jnp.int32)
                    for m in range(i):
                        if m == j:
                            continue
                        beats = (g[m] >= g[j]) if m < j else (g[m] > g[j])
                        rank = rank + beats.astype(jnp.int32)
                    sj = jnp.where(rank < MOBA_TOPK, sj, NEG)
                pieces.append(sj)
            s = pieces[0] if i == 0 else jnp.concatenate(pieces, axis=1)
            m_row = jnp.max(s, axis=-1, keepdims=True)
            p = jnp.exp(s - m_row)
            l_row = jnp.sum(p, axis=-1, keepdims=True)
            o = _dot(p.astype(BF16), vh[:(i + 1) * MOBA_BLOCK]) / l_row
            outs[i] = o if outs[i] is None else outs[i] + o
    for i in range(nb):
        o_ref[i * MOBA_BLOCK:(i + 1) * MOBA_BLOCK, :] = outs[i].astype(BF16)


def _attn(q2, k2, v2, bsz, seq):
    spec = pl.BlockSpec((seq, LANES), lambda b, hp: (b, hp))
    return pl.pallas_call(
        _attn_kernel,
        grid=(bsz, ATTN_WIDTH // LANES),
        in_specs=[spec, spec, spec],
        out_specs=spec,
        out_shape=jax.ShapeDtypeStruct(q2.shape, BF16),
        compiler_params=pltpu.CompilerParams(
            dimension_semantics=("arbitrary", "arbitrary"), vmem_limit_bytes=VMEM_LIMIT),
        name="attn",
    )(q2, k2, v2)


def _tail_kernel(x_ref, c_ref, a_ref, p_ref, wout_ref, gf_ref, wup_ref, wdn_ref,
                 gp_ref, wg_ref, wp_ref, gfin_ref, o_ref):
    h = (x_ref[...]
         + _dot(c_ref[...], wout_ref[:CONV_CH, :])
         + _dot(a_ref[...], wout_ref[CONV_CH:, :]))
    hn = _rmsnorm(h, gf_ref[...]).astype(BF16)
    gu = _dot(hn, wup_ref[...])
    act = (jax.nn.silu(gu[:, :D_FF]) * gu[:, D_FF:]).astype(BF16)
    h = h + _dot(act, wdn_ref[...])
    hn = _rmsnorm(h, gp_ref[...]).astype(BF16)
    gate = jax.nn.sigmoid(_dot(hn, wg_ref[...]))
    h = h + gate * _dot(p_ref[...].astype(BF16), wp_ref[...])
    o_ref[...] = _rmsnorm(h, gfin_ref[...])


def _tail(x2, conv2, attn2, p2, wout, gf, wup, wdn, gp, wg, wp, gfin):
    t = x2.shape[0]
    rows = TAIL_ROWS
    row_spec = lambda width: pl.BlockSpec((rows, width), lambda i: (i, 0))
    full = lambda a: pl.BlockSpec(a.shape, lambda i: (0, 0), pipeline_mode=pl.Buffered(1))
    return pl.pallas_call(
        _tail_kernel,
        grid=(t // rows,),
        in_specs=[row_spec(D_MODEL), row_spec(CONV_CH), row_spec(ATTN_WIDTH), row_spec(PLE_DIM),
                  full(wout), full(gf), full(wup), full(wdn), full(gp), full(wg), full(wp),
                  full(gfin)],
        out_specs=row_spec(D_MODEL),
        out_shape=jax.ShapeDtypeStruct((t, D_MODEL), F32),
        compiler_params=pltpu.CompilerParams(
            dimension_semantics=("arbitrary",), vmem_limit_bytes=VMEM_LIMIT),
        name="tail",
    )(x2, conv2, attn2, p2, wout, gf, wup, wdn, gp, wg, wp, gfin)


def _rope_lane_freq():
    inv_freq = ROPE_THETA ** (-jnp.arange(0, ROPE_DIM, 2, dtype=F32) / ROPE_DIM)
    d = jnp.arange(LANES) % HEAD_DIM
    return jnp.where(d < ROPE_DIM, inv_freq[d % ROPE_HALF], 0.0).reshape(1, LANES).astype(F32)


def kernel(x, p, positions, norm_mix_g, w_in, conv_w, conv_b, conv_ln_g, conv_ln_b, w_out,
           norm_ffn_g, w_ffn_up, w_ffn_down, norm_ple_g, w_ple_gate, w_ple_proj, final_norm_g):
    bsz, seq, _ = x.shape
    t = bsz * seq
    depth = w_in.shape[0]
    row = lambda a: a.reshape(1, -1)
    freq = _rope_lane_freq()
    pos2 = positions.reshape(t, 1)
    h = x.reshape(t, D_MODEL)
    for i in range(depth):
        u, q, k, v = _in_proj(h, pos2, freq, row(norm_mix_g[i]), w_in[i].astype(BF16))
        conv = _conv(u.reshape(bsz, seq, CONV_CH), conv_w[i], row(conv_b[i]),
                     row(conv_ln_g[i]), row(conv_ln_b[i])).reshape(t, CONV_CH)
        attn = _attn(q, k, v, bsz, seq)
        assert depth == 1
        h = _tail(h, conv, attn, p[i].reshape(t, PLE_DIM), w_out[i].astype(BF16),
                  row(norm_ffn_g[i]), w_ffn_up[i].astype(BF16), w_ffn_down[i].astype(BF16),
                  row(norm_ple_g[i]), w_ple_gate[i].astype(BF16), w_ple_proj[i].astype(BF16),
                  row(final_norm_g))
    return h.reshape(bsz, seq, D_MODEL)
```

```python
import functools

import jax
import jax.numpy as jnp
from jax import lax
from jax.experimental import pallas as pl
from jax.experimental.pallas import tpu as pltpu

D_MODEL = 1024
CONV_CH = 512
ATTN_HEADS = 8
HEAD_DIM = 64
ATTN_WIDTH = ATTN_HEADS * HEAD_DIM
IN_WIDTH = 2 * CONV_CH + 3 * ATTN_WIDTH
CONV_KERNEL = 31
MOBA_BLOCK = 256
MOBA_TOPK = 3
ROPE_THETA = 500000.0
ROPE_DIM = HEAD_DIM // 4
ROPE_HALF = ROPE_DIM // 2
D_FF = -(-8 * D_MODEL // (3 * 256)) * 256
PLE_DIM = 256
EPS = 1e-6

LANES = 128
SUBLANES = 8
HEADS_PER_STEP = LANES // HEAD_DIM
CONV_PAD = 32
CONV_ROWS = 64
IN_ROWS = 512
TAIL_ROWS = 256
VMEM_LIMIT = 56 * 1024 * 1024

F32 = jnp.float32
BF16 = jnp.bfloat16
NEG = float(jnp.finfo(jnp.float32).min)
POS = float(jnp.finfo(jnp.float32).max)


def _rmsnorm(x, g):
    return x * lax.rsqrt(jnp.mean(x * x, axis=-1, keepdims=True) + EPS) * g


def _dot(a, b):
    return jnp.dot(a, b, preferred_element_type=F32)


def _dot_nt(a, b):
    return lax.dot_general(a, b, (((1,), (1,)), ((), ())), preferred_element_type=F32)


def _in_proj_kernel(x_ref, pos_ref, freq_ref, g_ref, w_ref, u_ref, q_ref, k_ref, v_ref):
    hn = _rmsnorm(x_ref[...], g_ref[...]).astype(BF16)
    z = _dot(hn, w_ref[...])
    u_ref[...] = z[:, :CONV_CH] * jax.nn.sigmoid(z[:, CONV_CH:2 * CONV_CH])

    ang = pos_ref[...].astype(F32) * freq_ref[...]
    cos = jnp.cos(ang)
    sin = jnp.sin(ang)
    d = lax.broadcasted_iota(jnp.int32, (1, LANES), 1) & (HEAD_DIM - 1)
    sin_from_hi = jnp.where(d < ROPE_HALF, -sin, 0.0)
    sin_from_lo = jnp.where(d >= ROPE_HALF, sin, 0.0)

    def rope(t):
        return (t * cos
                + pltpu.roll(t, ROPE_HALF, 1) * sin_from_lo
                + pltpu.roll(t, LANES - ROPE_HALF, 1) * sin_from_hi)

    q0 = 2 * CONV_CH
    k0 = q0 + ATTN_WIDTH
    v0 = k0 + ATTN_WIDTH
    scale = HEAD_DIM ** -0.5
    for c in range(ATTN_WIDTH // LANES):
        sl = slice(c * LANES, (c + 1) * LANES)
        q_ref[:, sl] = (rope(z[:, q0 + c * LANES:q0 + (c + 1) * LANES]) * scale).astype(BF16)
        k_ref[:, sl] = rope(z[:, k0 + c * LANES:k0 + (c + 1) * LANES]).astype(BF16)
    v_ref[...] = z[:, v0:v0 + ATTN_WIDTH].astype(BF16)


def _in_proj(x2, pos2, freq, g, w):
    t = x2.shape[0]
    rows = IN_ROWS
    row_spec = lambda width: pl.BlockSpec((rows, width), lambda i: (i, 0))
    full = lambda a: pl.BlockSpec(a.shape, lambda i: (0, 0))
    return pl.pallas_call(
        _in_proj_kernel,
        grid=(t // rows,),
        in_specs=[row_spec(D_MODEL), row_spec(1), full(freq), full(g), full(w)],
        out_specs=[row_spec(CONV_CH), row_spec(ATTN_WIDTH), row_spec(ATTN_WIDTH), row_spec(ATTN_WIDTH)],
        out_shape=[jax.ShapeDtypeStruct((t, CONV_CH), F32),
                   jax.ShapeDtypeStruct((t, ATTN_WIDTH), BF16),
                   jax.ShapeDtypeStruct((t, ATTN_WIDTH), BF16),
                   jax.ShapeDtypeStruct((t, ATTN_WIDTH), BF16)],
        compiler_params=pltpu.CompilerParams(
            dimension_semantics=("arbitrary",), vmem_limit_bytes=VMEM_LIMIT),
        name="in_proj",
    )(x2, pos2, freq, g, w)


def _conv_kernel(u_ref, w_ref, b_ref, lg_ref, lb_ref, o_ref, pad_ref):
    seq = u_ref.shape[0]
    pad_ref[0:CONV_PAD, :] = jnp.zeros((CONV_PAD, CONV_CH), F32)
    pad_ref[CONV_PAD:, :] = u_ref[...]
    first = CONV_PAD - (CONV_KERNEL - 1)

    def step(c, carry):
        t0 = pl.multiple_of(c * CONV_ROWS, CONV_ROWS)
        win = pad_ref[pl.ds(t0, CONV_ROWS + CONV_PAD), :]
        acc = jnp.broadcast_to(b_ref[...], (CONV_ROWS, CONV_CH))
        for phase in range(SUBLANES):
            shifted = win if phase == 0 else win[phase:phase + CONV_ROWS + CONV_PAD - SUBLANES]
            for base in range(0, CONV_PAD + 1, SUBLANES):
                j = base + phase - first
                if 0 <= j < CONV_KERNEL:
                    acc = acc + w_ref[j:j + 1, :] * shifted[base:base + CONV_ROWS]
        mu = jnp.mean(acc, axis=-1, keepdims=True)
        dev = acc - mu
        var = jnp.mean(dev * dev, axis=-1, keepdims=True)
        y = dev * lax.rsqrt(var + EPS) * lg_ref[...] + lb_ref[...]
        o_ref[pl.ds(t0, CONV_ROWS), :] = (y * jax.nn.sigmoid(y)).astype(BF16)
        return carry

    lax.fori_loop(0, seq // CONV_ROWS, step, 0)


def _conv(u3, w, b, lg, lb):
    bsz, seq, _ = u3.shape
    full = lambda a: pl.BlockSpec(a.shape, lambda i: (0, 0))
    return pl.pallas_call(
        _conv_kernel,
        grid=(bsz,),
        in_specs=[pl.BlockSpec((None, seq, CONV_CH), lambda i: (i, 0, 0)),
                  full(w), full(b), full(lg), full(lb)],
        out_specs=pl.BlockSpec((None, seq, CONV_CH), lambda i: (i, 0, 0)),
        out_shape=jax.ShapeDtypeStruct((bsz, seq, CONV_CH), BF16),
        scratch_shapes=[pltpu.VMEM((seq + CONV_PAD, CONV_CH), F32)],
        compiler_params=pltpu.CompilerParams(
            dimension_semantics=("arbitrary",), vmem_limit_bytes=VMEM_LIMIT),
        name="conv",
    )(u3, w, b, lg, lb)


def _attn_kernel(q_ref, k_ref, v_ref, o_ref, ot_ref):
    seq = q_ref.shape[0]
    nb = seq // MOBA_BLOCK
    q = q_ref[...]
    k = k_ref[...]
    vt = v_ref[...].astype(F32).T.astype(BF16)
    lane = lax.broadcasted_iota(jnp.int32, (1, LANES), 1)
    kmean = jnp.mean(k.astype(F32).reshape(nb, MOBA_BLOCK, LANES), axis=1)
    key_row = lax.broadcasted_iota(jnp.int32, (MOBA_BLOCK, MOBA_BLOCK), 0)
    q_col = lax.broadcasted_iota(jnp.int32, (MOBA_BLOCK, MOBA_BLOCK), 1)
    causal = key_row <= q_col
    blk = lax.broadcasted_iota(jnp.int32, (nb, MOBA_BLOCK), 0)
    first_ranked = MOBA_TOPK + 1

    for h in range(HEADS_PER_STEP):
        in_head = (lane >= h * HEAD_DIM) & (lane < (h + 1) * HEAD_DIM)
        kh = jnp.where(in_head, k, jnp.zeros_like(k))
        kmh = jnp.where(in_head, kmean, 0.0).astype(BF16)
        gate_t = _dot_nt(kmh, q[first_ranked * MOBA_BLOCK:])
        for i in range(nb):
            qi = q[i * MOBA_BLOCK:(i + 1) * MOBA_BLOCK]
            st = _dot_nt(kh[:(i + 1) * MOBA_BLOCK], qi)
            if i >= first_ranked:
                c0 = (i - first_ranked) * MOBA_BLOCK
                gi = gate_t[:, c0:c0 + MOBA_BLOCK]
                rank = jnp.zeros((nb, MOBA_BLOCK), jnp.int32)
                for m in range(i):
                    gm = gi[m:m + 1, :]
                    beats = (gm > gi) | ((gm == gi) & (blk > m))
                    rank = rank + jnp.where(beats, 1, 0)
                cap = jnp.where(rank < MOBA_TOPK, POS, NEG)
            pieces = []
            for j in range(i + 1):
                sj = st[j * MOBA_BLOCK:(j + 1) * MOBA_BLOCK]
                if j == i:
                    sj = jnp.where(causal, sj, NEG)
                elif i >= first_ranked:
                    sj = jnp.minimum(sj, cap[j:j + 1, :])
                pieces.append(sj)
            st = pieces[0] if i == 0 else jnp.concatenate(pieces, axis=0)
            m_col = jnp.max(st, axis=0, keepdims=True)
            p = jnp.exp(st - m_col)
            l_col = jnp.sum(p, axis=0, keepdims=True)
            ot = _dot(vt[h * HEAD_DIM:(h + 1) * HEAD_DIM, :(i + 1) * MOBA_BLOCK], p.astype(BF16))
            ot_ref[h * HEAD_DIM:(h + 1) * HEAD_DIM, i * MOBA_BLOCK:(i + 1) * MOBA_BLOCK] = ot / l_col
    o_ref[...] = ot_ref[...].T.astype(BF16)


def _attn(q2, k2, v2, bsz, seq):
    spec = pl.BlockSpec((seq, LANES), lambda b, hp: (b, hp))
    return pl.pallas_call(
        _attn_kernel,
        grid=(bsz, ATTN_WIDTH // LANES),
        in_specs=[spec, spec, spec],
        out_specs=spec,
        out_shape=jax.ShapeDtypeStruct(q2.shape, BF16),
        scratch_shapes=[pltpu.VMEM((LANES, seq), F32)],
        compiler_params=pltpu.CompilerParams(
            dimension_semantics=("arbitrary", "arbitrary"), vmem_limit_bytes=VMEM_LIMIT),
        name="attn",
    )(q2, k2, v2)


def _tail_kernel(x_ref, c_ref, a_ref, p_ref, wout_ref, gf_ref, wup_ref, wdn_ref,
                 gp_ref, wg_ref, wp_ref, gfin_ref, o_ref):
    h = (x_ref[...]
         + _dot(c_ref[...], wout_ref[:CONV_CH, :])
         + _dot(a_ref[...], wout_ref[CONV_CH:, :]))
    hn = _rmsnorm(h, gf_ref[...]).astype(BF16)
    gu = _dot(hn, wup_ref[...])
    act = (jax.nn.silu(gu[:, :D_FF]) * gu[:, D_FF:]).astype(BF16)
    h = h + _dot(act, wdn_ref[...])
    hn = _rmsnorm(h, gp_ref[...]).astype(BF16)
    gate = jax.nn.sigmoid(_dot(hn, wg_ref[...]))
    h = h + gate * _dot(p_ref[...].astype(BF16), wp_ref[...])
    o_ref[...] = _rmsnorm(h, gfin_ref[...])


def _tail(x2, conv2, attn2, p2, wout, gf, wup, wdn, gp, wg, wp, gfin):
    t = x2.shape[0]
    rows = TAIL_ROWS
    row_spec = lambda width: pl.BlockSpec((rows, width), lambda i: (i, 0))
    full = lambda a: pl.BlockSpec(a.shape, lambda i: (0, 0), pipeline_mode=pl.Buffered(1))
    return pl.pallas_call(
        _tail_kernel,
        grid=(t // rows,),
        in_specs=[row_spec(D_MODEL), row_spec(CONV_CH), row_spec(ATTN_WIDTH), row_spec(PLE_DIM),
                  full(wout), full(gf), full(wup), full(wdn), full(gp), full(wg), full(wp),
                  full(gfin)],
        out_specs=row_spec(D_MODEL),
        out_shape=jax.ShapeDtypeStruct((t, D_MODEL), F32),
        compiler_params=pltpu.CompilerParams(
            dimension_semantics=("arbitrary",), vmem_limit_bytes=VMEM_LIMIT),
        name="tail",
    )(x2, conv2, attn2, p2, wout, gf, wup, wdn, gp, wg, wp, gfin)


def _rope_lane_freq():
    inv_freq = ROPE_THETA ** (-jnp.arange(0, ROPE_DIM, 2, dtype=F32) / ROPE_DIM)
    d = jnp.arange(LANES) % HEAD_DIM
    return jnp.where(d < ROPE_DIM, inv_freq[d % ROPE_HALF], 0.0).reshape(1, LANES).astype(F32)


def kernel(x, p, positions, norm_mix_g, w_in, conv_w, conv_b, conv_ln_g, conv_ln_b, w_out,
           norm_ffn_g, w_ffn_up, w_ffn_down, norm_ple_g, w_ple_gate, w_ple_proj, final_norm_g):
    bsz, seq, _ = x.shape
    t = bsz * seq
    depth = w_in.shape[0]
    row = lambda a: a.reshape(1, -1)
    freq = _rope_lane_freq()
    pos2 = positions.reshape(t, 1)
    h = x.reshape(t, D_MODEL)
    for i in range(depth):
        u, q, k, v = _in_proj(h, pos2, freq, row(norm_mix_g[i]), w_in[i].astype(BF16))
        conv = _conv(u.reshape(bsz, seq, CONV_CH), conv_w[i], row(conv_b[i]),
                     row(conv_ln_g[i]), row(conv_ln_b[i])).reshape(t, CONV_CH)
        attn = _attn(q, k, v, bsz, seq)
        assert depth == 1
        h = _tail(h, conv, attn, p[i].reshape(t, PLE_DIM), w_out[i].astype(BF16),
                  row(norm_ffn_g[i]), w_ffn_up[i].astype(BF16), w_ffn_down[i].astype(BF16),
                  row(norm_ple_g[i]), w_ple_gate[i].astype(BF16), w_ple_proj[i].astype(BF16),
                  row(final_norm_g))
    return h.reshape(bsz, seq, D_MODEL)
```

```python
import functools

import jax
import jax.numpy as jnp
from jax import lax
from jax.experimental import pallas as pl
from jax.experimental.pallas import tpu as pltpu

D_MODEL = 1024
CONV_CH = 512
ATTN_HEADS = 8
HEAD_DIM = 64
ATTN_WIDTH = ATTN_HEADS * HEAD_DIM
IN_WIDTH = 2 * CONV_CH + 3 * ATTN_WIDTH
CONV_KERNEL = 31
MOBA_BLOCK = 256
MOBA_TOPK = 3
ROPE_THETA = 500000.0
ROPE_DIM = HEAD_DIM // 4
ROPE_HALF = ROPE_DIM // 2
D_FF = -(-8 * D_MODEL // (3 * 256)) * 256
PLE_DIM = 256
EPS = 1e-6

LANES = 128
SUBLANES = 8
BF16_ROWS = 16
LOG2E = 1.4426950408889634
HEADS_PER_STEP = LANES // HEAD_DIM
CONV_PAD = 32
CONV_ROWS = 64
IN_ROWS = 512
TAIL_ROWS = 256
VMEM_LIMIT = 56 * 1024 * 1024

F32 = jnp.float32
BF16 = jnp.bfloat16
NEG = float(jnp.finfo(jnp.float32).min)
POS = float(jnp.finfo(jnp.float32).max)


def _rmsnorm(x, g):
    return x * lax.rsqrt(jnp.mean(x * x, axis=-1, keepdims=True) + EPS) * g


def _dot(a, b):
    return jnp.dot(a, b, preferred_element_type=F32)


def _dot_nt(a, b):
    return lax.dot_general(a, b, (((1,), (1,)), ((), ())), preferred_element_type=F32)


def _in_proj_kernel(x_ref, pos_ref, freq_ref, g_ref, w_ref, u_ref, q_ref, k_ref, v_ref):
    hn = _rmsnorm(x_ref[...], g_ref[...]).astype(BF16)
    z = _dot(hn, w_ref[...])
    u_ref[...] = z[:, :CONV_CH] * jax.nn.sigmoid(z[:, CONV_CH:2 * CONV_CH])

    ang = pos_ref[...].astype(F32) * freq_ref[...]
    cos = jnp.cos(ang)
    sin = jnp.sin(ang)
    d = lax.broadcasted_iota(jnp.int32, (1, LANES), 1) & (HEAD_DIM - 1)
    sin_from_hi = jnp.where(d < ROPE_HALF, -sin, 0.0)
    sin_from_lo = jnp.where(d >= ROPE_HALF, sin, 0.0)

    def rope(t):
        return (t * cos
                + pltpu.roll(t, ROPE_HALF, 1) * sin_from_lo
                + pltpu.roll(t, LANES - ROPE_HALF, 1) * sin_from_hi)

    q0 = 2 * CONV_CH
    k0 = q0 + ATTN_WIDTH
    v0 = k0 + ATTN_WIDTH
    scale = HEAD_DIM ** -0.5 * LOG2E
    for c in range(ATTN_WIDTH // LANES):
        sl = slice(c * LANES, (c + 1) * LANES)
        q_ref[:, sl] = (rope(z[:, q0 + c * LANES:q0 + (c + 1) * LANES]) * scale).astype(BF16)
        k_ref[:, sl] = rope(z[:, k0 + c * LANES:k0 + (c + 1) * LANES]).astype(BF16)
    v_ref[...] = z[:, v0:v0 + ATTN_WIDTH].astype(BF16)


def _in_proj(x2, pos2, freq, g, w):
    t = x2.shape[0]
    rows = IN_ROWS
    row_spec = lambda width: pl.BlockSpec((rows, width), lambda i: (i, 0))
    full = lambda a: pl.BlockSpec(a.shape, lambda i: (0, 0))
    return pl.pallas_call(
        _in_proj_kernel,
        grid=(t // rows,),
        in_specs=[row_spec(D_MODEL), row_spec(1), full(freq), full(g), full(w)],
        out_specs=[row_spec(CONV_CH), row_spec(ATTN_WIDTH), row_spec(ATTN_WIDTH), row_spec(ATTN_WIDTH)],
        out_shape=[jax.ShapeDtypeStruct((t, CONV_CH), F32),
                   jax.ShapeDtypeStruct((t, ATTN_WIDTH), BF16),
                   jax.ShapeDtypeStruct((t, ATTN_WIDTH), BF16),
                   jax.ShapeDtypeStruct((t, ATTN_WIDTH), BF16)],
        compiler_params=pltpu.CompilerParams(
            dimension_semantics=("arbitrary",), vmem_limit_bytes=VMEM_LIMIT),
        name="in_proj",
    )(x2, pos2, freq, g, w)


def _conv_kernel(u_ref, w_ref, b_ref, lg_ref, lb_ref, o_ref, pad_ref):
    seq = u_ref.shape[0]
    pad_ref[0:CONV_PAD, :] = jnp.zeros((CONV_PAD, CONV_CH), F32)
    pad_ref[CONV_PAD:, :] = u_ref[...]
    first = CONV_PAD - (CONV_KERNEL - 1)

    def step(c, carry):
        t0 = pl.multiple_of(c * CONV_ROWS, CONV_ROWS)
        win_rows = CONV_ROWS + CONV_PAD
        accs = []
        for g in range(CONV_CH // LANES):
            lanes = slice(g * LANES, (g + 1) * LANES)
            win = pad_ref[pl.ds(t0, win_rows), lanes]
            acc = jnp.broadcast_to(b_ref[:, lanes], (CONV_ROWS, LANES))
            for phase in range(SUBLANES):
                shifted = win if phase == 0 else pltpu.roll(win, win_rows - phase, 0)
                for base in range(0, CONV_PAD + 1, SUBLANES):
                    j = base + phase - first
                    if 0 <= j < CONV_KERNEL:
                        acc = acc + w_ref[j:j + 1, lanes] * shifted[base:base + CONV_ROWS]
            accs.append(acc)
        acc = jnp.concatenate(accs, axis=1)
        mu = jnp.mean(acc, axis=-1, keepdims=True)
        dev = acc - mu
        var = jnp.mean(dev * dev, axis=-1, keepdims=True)
        y = dev * lax.rsqrt(var + EPS) * lg_ref[...] + lb_ref[...]
        o_ref[pl.ds(t0, CONV_ROWS), :] = (y * jax.nn.sigmoid(y)).astype(BF16)
        return carry

    lax.fori_loop(0, seq // CONV_ROWS, step, 0)


def _conv(u3, w, b, lg, lb):
    bsz, seq, _ = u3.shape
    full = lambda a: pl.BlockSpec(a.shape, lambda i: (0, 0))
    return pl.pallas_call(
        _conv_kernel,
        grid=(bsz,),
        in_specs=[pl.BlockSpec((None, seq, CONV_CH), lambda i: (i, 0, 0)),
                  full(w), full(b), full(lg), full(lb)],
        out_specs=pl.BlockSpec((None, seq, CONV_CH), lambda i: (i, 0, 0)),
        out_shape=jax.ShapeDtypeStruct((bsz, seq, CONV_CH), BF16),
        scratch_shapes=[pltpu.VMEM((seq + CONV_PAD, CONV_CH), F32)],
        compiler_params=pltpu.CompilerParams(
            dimension_semantics=("arbitrary",), vmem_limit_bytes=VMEM_LIMIT),
        name="conv",
    )(u3, w, b, lg, lb)


def _attn_kernel(q_ref, k_ref, v_ref, o_ref, ot_ref):
    seq = q_ref.shape[0]
    nb = seq // MOBA_BLOCK
    q = q_ref[...]
    k = k_ref[...]
    vt = v_ref[...].astype(F32).T.astype(BF16)
    lane = lax.broadcasted_iota(jnp.int32, (1, LANES), 1)
    kmean = jnp.mean(k.astype(F32).reshape(nb, MOBA_BLOCK, LANES), axis=1)
    key_row = lax.broadcasted_iota(jnp.int32, (MOBA_BLOCK, MOBA_BLOCK), 0)
    q_col = lax.broadcasted_iota(jnp.int32, (MOBA_BLOCK, MOBA_BLOCK), 1)
    causal = key_row <= q_col
    blk = lax.broadcasted_iota(jnp.int32, (nb, MOBA_BLOCK), 0)
    first_ranked = MOBA_TOPK + 1
    ones = jnp.ones((BF16_ROWS, seq), BF16)

    kh, vth, gate_t = [], [], []
    for h in range(HEADS_PER_STEP):
        in_head = (lane >= h * HEAD_DIM) & (lane < (h + 1) * HEAD_DIM)
        kh.append(jnp.where(in_head, k, jnp.zeros_like(k)))
        vth.append(jnp.concatenate([vt[h * HEAD_DIM:(h + 1) * HEAD_DIM], ones], axis=0))
        kmh = jnp.where(in_head, kmean, 0.0).astype(BF16)
        gate_t.append(_dot_nt(kmh, q[first_ranked * MOBA_BLOCK:]))

    def scores(h, i):
        qi = q[i * MOBA_BLOCK:(i + 1) * MOBA_BLOCK]
        return _dot_nt(kh[h][:(i + 1) * MOBA_BLOCK], qi)

    def finish(h, i, st):
        if i >= first_ranked:
            c0 = (i - first_ranked) * MOBA_BLOCK
            gi = gate_t[h][:, c0:c0 + MOBA_BLOCK]
            rank = jnp.zeros((nb, MOBA_BLOCK), jnp.int32)
            for m in range(i):
                gm = gi[m:m + 1, :]
                beats = (gm > gi) | ((gm == gi) & (blk > m))
                rank = rank + jnp.where(beats, 1, 0)
            cap = jnp.where(rank < MOBA_TOPK, POS, NEG)
        pieces = []
        for j in range(i + 1):
            sj = st[j * MOBA_BLOCK:(j + 1) * MOBA_BLOCK]
            if j == i:
                sj = jnp.where(causal, sj, NEG)
            elif i >= first_ranked:
                sj = jnp.minimum(sj, cap[j:j + 1, :])
            pieces.append(sj)
        st = pieces[0] if i == 0 else jnp.concatenate(pieces, axis=0)
        m_col = jnp.max(st, axis=0, keepdims=True)
        p = jnp.exp2(st - m_col).astype(BF16)
        ot = _dot(vth[h][:, :(i + 1) * MOBA_BLOCK], p)
        ot_ref[h * HEAD_DIM:(h + 1) * HEAD_DIM, i * MOBA_BLOCK:(i + 1) * MOBA_BLOCK] = (
            ot[:HEAD_DIM] / ot[HEAD_DIM:HEAD_DIM + 1])

    units = [(h, i) for i in range(nb) for h in range(HEADS_PER_STEP)]
    st_next = scores(*units[0])
    for n, (h, i) in enumerate(units):
        st = st_next
        if n + 1 < len(units):
            st_next = scores(*units[n + 1])
        finish(h, i, st)
    o_ref[...] = ot_ref[...].T.astype(BF16)


def _attn(q2, k2, v2, bsz, seq):
    spec = pl.BlockSpec((seq, LANES), lambda b, hp: (b, hp))
    return pl.pallas_call(
        _attn_kernel,
        grid=(bsz, ATTN_WIDTH // LANES),
        in_specs=[spec, spec, spec],
        out_specs=spec,
        out_shape=jax.ShapeDtypeStruct(q2.shape, BF16),
        scratch_shapes=[pltpu.VMEM((LANES, seq), F32)],
        compiler_params=pltpu.CompilerParams(
            dimension_semantics=("arbitrary", "arbitrary"), vmem_limit_bytes=VMEM_LIMIT),
        name="attn",
    )(q2, k2, v2)


def _tail_kernel(x_ref, c_ref, a_ref, p_ref, wout_ref, gf_ref, wup_ref, wdn_ref,
                 gp_ref, wg_ref, wp_ref, gfin_ref, o_ref):
    h = (x_ref[...]
         + _dot(c_ref[...], wout_ref[:CONV_CH, :])
         + _dot(a_ref[...], wout_ref[CONV_CH:, :]))
    hn = _rmsnorm(h, gf_ref[...]).astype(BF16)
    gu = _dot(hn, wup_ref[...])
    act = (jax.nn.silu(gu[:, :D_FF]) * gu[:, D_FF:]).astype(BF16)
    h = h + _dot(act, wdn_ref[...])
    hn = _rmsnorm(h, gp_ref[...]).astype(BF16)
    gate = jax.nn.sigmoid(_dot(hn, wg_ref[...]))
    h = h + gate * _dot(p_ref[...].astype(BF16), wp_ref[...])
    o_ref[...] = _rmsnorm(h, gfin_ref[...])


def _tail(x2, conv2, attn2, p2, wout, gf, wup, wdn, gp, wg, wp, gfin):
    t = x2.shape[0]
    rows = TAIL_ROWS
    row_spec = lambda width: pl.BlockSpec((rows, width), lambda i: (i, 0))
    full = lambda a: pl.BlockSpec(a.shape, lambda i: (0, 0), pipeline_mode=pl.Buffered(1))
    return pl.pallas_call(
        _tail_kernel,
        grid=(t // rows,),
        in_specs=[row_spec(D_MODEL), row_spec(CONV_CH), row_spec(ATTN_WIDTH), row_spec(PLE_DIM),
                  full(wout), full(gf), full(wup), full(wdn), full(gp), full(wg), full(wp),
                  full(gfin)],
        out_specs=row_spec(D_MODEL),
        out_shape=jax.ShapeDtypeStruct((t, D_MODEL), F32),
        compiler_params=pltpu.CompilerParams(
            dimension_semantics=("arbitrary",), vmem_limit_bytes=VMEM_LIMIT),
        name="tail",
    )(x2, conv2, attn2, p2, wout, gf, wup, wdn, gp, wg, wp, gfin)


def _rope_lane_freq():
    inv_freq = ROPE_THETA ** (-jnp.arange(0, ROPE_DIM, 2, dtype=F32) / ROPE_DIM)
    d = jnp.arange(LANES) % HEAD_DIM
    return jnp.where(d < ROPE_DIM, inv_freq[d % ROPE_HALF], 0.0).reshape(1, LANES).astype(F32)


def kernel(x, p, positions, norm_mix_g, w_in, conv_w, conv_b, conv_ln_g, conv_ln_b, w_out,
           norm_ffn_g, w_ffn_up, w_ffn_down, norm_ple_g, w_ple_gate, w_ple_proj, final_norm_g):
    bsz, seq, _ = x.shape
    t = bsz * seq
    depth = w_in.shape[0]
    row = lambda a: a.reshape(1, -1)
    freq = _rope_lane_freq()
    pos2 = positions.reshape(t, 1)
    h = x.reshape(t, D_MODEL)
    for i in range(depth):
        u, q, k, v = _in_proj(h, pos2, freq, row(norm_mix_g[i]), w_in[i].astype(BF16))
        conv = _conv(u.reshape(bsz, seq, CONV_CH), conv_w[i], row(conv_b[i]),
                     row(conv_ln_g[i]), row(conv_ln_b[i])).reshape(t, CONV_CH)
        attn = _attn(q, k, v, bsz, seq)
        assert depth == 1
        h = _tail(h, conv, attn, p[i].reshape(t, PLE_DIM), w_out[i].astype(BF16),
                  row(norm_ffn_g[i]), w_ffn_up[i].astype(BF16), w_ffn_down[i].astype(BF16),
                  row(norm_ple_g[i]), w_ple_gate[i].astype(BF16), w_ple_proj[i].astype(BF16),
                  row(final_norm_g))
    return h.reshape(bsz, seq, D_MODEL)
```

```python
import functools

import jax
import jax.numpy as jnp
from jax import lax
from jax.experimental import pallas as pl
from jax.experimental.pallas import tpu as pltpu

D_MODEL = 1024
CONV_CH = 512
ATTN_HEADS = 8
HEAD_DIM = 64
ATTN_WIDTH = ATTN_HEADS * HEAD_DIM
IN_WIDTH = 2 * CONV_CH + 3 * ATTN_WIDTH
CONV_KERNEL = 31
MOBA_BLOCK = 256
MOBA_TOPK = 3
ROPE_THETA = 500000.0
ROPE_DIM = HEAD_DIM // 4
ROPE_HALF = ROPE_DIM // 2
D_FF = -(-8 * D_MODEL // (3 * 256)) * 256
PLE_DIM = 256
EPS = 1e-6

LANES = 128
SUBLANES = 8
BF16_ROWS = 16
LOG2E = 1.4426950408889634
HEADS_PER_STEP = LANES // HEAD_DIM
CONV_PAD = 32
CONV_ROWS = 64
IN_ROWS = 512
TAIL_ROWS = 256
VMEM_LIMIT = 56 * 1024 * 1024

F32 = jnp.float32
BF16 = jnp.bfloat16
NEG = float(jnp.finfo(jnp.float32).min)
POS = float(jnp.finfo(jnp.float32).max)


def _rmsnorm(x, g):
    return x * lax.rsqrt(jnp.mean(x * x, axis=-1, keepdims=True) + EPS) * g


def _dot(a, b):
    return jnp.dot(a, b, preferred_element_type=F32)


def _dot_nt(a, b):
    return lax.dot_general(a, b, (((1,), (1,)), ((), ())), preferred_element_type=F32)


def _in_proj_kernel(x_ref, pos_ref, freq_ref, g_ref, w_ref, u_ref, q_ref, k_ref, v_ref):
    hn = _rmsnorm(x_ref[...], g_ref[...]).astype(BF16)
    z = _dot(hn, w_ref[...])
    u_ref[...] = z[:, :CONV_CH] * jax.nn.sigmoid(z[:, CONV_CH:2 * CONV_CH])

    ang = pos_ref[...].astype(F32) * freq_ref[...]
    cos = jnp.cos(ang)
    sin = jnp.sin(ang)
    d = lax.broadcasted_iota(jnp.int32, (1, LANES), 1) & (HEAD_DIM - 1)
    sin_from_hi = jnp.where(d < ROPE_HALF, -sin, 0.0)
    sin_from_lo = jnp.where(d >= ROPE_HALF, sin, 0.0)

    def rope(t):
        return (t * cos
                + pltpu.roll(t, ROPE_HALF, 1) * sin_from_lo
                + pltpu.roll(t, LANES - ROPE_HALF, 1) * sin_from_hi)

    q0 = 2 * CONV_CH
    k0 = q0 + ATTN_WIDTH
    v0 = k0 + ATTN_WIDTH
    scale = HEAD_DIM ** -0.5 * LOG2E
    for c in range(ATTN_WIDTH // LANES):
        sl = slice(c * LANES, (c + 1) * LANES)
        q_ref[:, sl] = (rope(z[:, q0 + c * LANES:q0 + (c + 1) * LANES]) * scale).astype(BF16)
        k_ref[:, sl] = rope(z[:, k0 + c * LANES:k0 + (c + 1) * LANES]).astype(BF16)
    v_ref[...] = z[:, v0:v0 + ATTN_WIDTH].astype(BF16)


def _in_proj(x2, pos2, freq, g, w):
    t = x2.shape[0]
    rows = IN_ROWS
    row_spec = lambda width: pl.BlockSpec((rows, width), lambda i: (i, 0))
    full = lambda a: pl.BlockSpec(a.shape, lambda i: (0, 0))
    return pl.pallas_call(
        _in_proj_kernel,
        grid=(t // rows,),
        in_specs=[row_spec(D_MODEL), row_spec(1), full(freq), full(g), full(w)],
        out_specs=[row_spec(CONV_CH), row_spec(ATTN_WIDTH), row_spec(ATTN_WIDTH), row_spec(ATTN_WIDTH)],
        out_shape=[jax.ShapeDtypeStruct((t, CONV_CH), F32),
                   jax.ShapeDtypeStruct((t, ATTN_WIDTH), BF16),
                   jax.ShapeDtypeStruct((t, ATTN_WIDTH), BF16),
                   jax.ShapeDtypeStruct((t, ATTN_WIDTH), BF16)],
        compiler_params=pltpu.CompilerParams(
            dimension_semantics=("arbitrary",), vmem_limit_bytes=VMEM_LIMIT),
        name="in_proj",
    )(x2, pos2, freq, g, w)


def _conv_rows(window, w_ref, b_ref, lg_ref, lb_ref, not_before):
    first = CONV_PAD - (CONV_KERNEL - 1)
    win_rows = CONV_ROWS + CONV_PAD
    a = not_before[:BF16_ROWS, :LANES].astype(F32)[0:1, :]
    hold = a - a
    accs = []
    for g in range(CONV_CH // LANES):
        lanes = slice(g * LANES, (g + 1) * LANES)
        win = window(lanes)
        acc = jnp.broadcast_to(b_ref[:, lanes] + hold, (CONV_ROWS, LANES))
        for phase in range(SUBLANES):
            shifted = win if phase == 0 else pltpu.roll(win, win_rows - phase, 0)
            for base in range(0, CONV_PAD + 1, SUBLANES):
                j = base + phase - first
                if 0 <= j < CONV_KERNEL:
                    acc = acc + w_ref[j:j + 1, lanes] * shifted[base:base + CONV_ROWS]
        accs.append(acc)
        hold = acc[0:1, :] - acc[0:1, :]
    acc = jnp.concatenate(accs, axis=1)
    mu = jnp.mean(acc, axis=-1, keepdims=True)
    dev = acc - mu
    var = jnp.mean(dev * dev, axis=-1, keepdims=True)
    y = dev * lax.rsqrt(var + EPS) * lg_ref[...] + lb_ref[...]
    return (y * jax.nn.sigmoid(y)).astype(BF16)


def _attn_kernel(q_ref, k_ref, v_ref, o_ref, ot_ref):
    seq = q_ref.shape[0]
    nb = seq // MOBA_BLOCK
    q = q_ref[...]
    k = k_ref[...]
    vt = v_ref[...].astype(F32).T.astype(BF16)
    lane = lax.broadcasted_iota(jnp.int32, (1, LANES), 1)
    kmean = jnp.mean(k.astype(F32).reshape(nb, MOBA_BLOCK, LANES), axis=1)
    key_row = lax.broadcasted_iota(jnp.int32, (MOBA_BLOCK, MOBA_BLOCK), 0)
    q_col = lax.broadcasted_iota(jnp.int32, (MOBA_BLOCK, MOBA_BLOCK), 1)
    causal = key_row <= q_col
    blk = lax.broadcasted_iota(jnp.int32, (nb, MOBA_BLOCK), 0)
    first_ranked = MOBA_TOPK + 1
    ones = jnp.ones((BF16_ROWS, seq), BF16)

    kh, vth, gate_t = [], [], []
    for h in range(HEADS_PER_STEP):
        in_head = (lane >= h * HEAD_DIM) & (lane < (h + 1) * HEAD_DIM)
        kh.append(jnp.where(in_head, k, jnp.zeros_like(k)))
        vth.append(jnp.concatenate([vt[h * HEAD_DIM:(h + 1) * HEAD_DIM], ones], axis=0))
        kmh = jnp.where(in_head, kmean, 0.0).astype(BF16)
        gate_t.append(_dot_nt(kmh, q[first_ranked * MOBA_BLOCK:]))

    def scores(h, i):
        qi = q[i * MOBA_BLOCK:(i + 1) * MOBA_BLOCK]
        return _dot_nt(kh[h][:(i + 1) * MOBA_BLOCK], qi)

    def finish(h, i, st):
        if i >= first_ranked:
            c0 = (i - first_ranked) * MOBA_BLOCK
            gi = gate_t[h][:, c0:c0 + MOBA_BLOCK]
            rank = jnp.zeros((nb, MOBA_BLOCK), jnp.int32)
            for m in range(i):
                gm = gi[m:m + 1, :]
                beats = (gm > gi) | ((gm == gi) & (blk > m))
                rank = rank + jnp.where(beats, 1, 0)
            cap = jnp.where(rank < MOBA_TOPK, POS, NEG)
        pieces = []
        for j in range(i + 1):
            sj = st[j * MOBA_BLOCK:(j + 1) * MOBA_BLOCK]
            if j == i:
                sj = jnp.where(causal, sj, NEG)
            elif i >= first_ranked:
                sj = jnp.minimum(sj, cap[j:j + 1, :])
            pieces.append(sj)
        st = pieces[0] if i == 0 else jnp.concatenate(pieces, axis=0)
        m_col = jnp.max(st, axis=0, keepdims=True)
        p = jnp.exp2(st - m_col).astype(BF16)
        ot = _dot(vth[h][:, :(i + 1) * MOBA_BLOCK], p)
        ot_ref[h * HEAD_DIM:(h + 1) * HEAD_DIM, i * MOBA_BLOCK:(i + 1) * MOBA_BLOCK] = (
            ot[:HEAD_DIM] / ot[HEAD_DIM:HEAD_DIM + 1])

    units = [(h, i) for i in range(nb) for h in range(HEADS_PER_STEP)]
    st_next = scores(*units[0])
    for n, (h, i) in enumerate(units):
        st = st_next
        if n + 1 < len(units):
            st_next = scores(*units[n + 1])
        finish(h, i, st)
    o_ref[...] = ot_ref[...].T.astype(BF16)


def _attn(q2, k2, v2, bsz, seq):
    spec = pl.BlockSpec((seq, LANES), lambda b, hp: (b, hp))
    return pl.pallas_call(
        _attn_kernel,
        grid=(bsz, ATTN_WIDTH // LANES),
        in_specs=[spec, spec, spec],
        out_specs=spec,
        out_shape=jax.ShapeDtypeStruct(q2.shape, BF16),
        scratch_shapes=[pltpu.VMEM((LANES, seq), F32)],
        compiler_params=pltpu.CompilerParams(
            dimension_semantics=("arbitrary", "arbitrary"), vmem_limit_bytes=VMEM_LIMIT),
        name="attn",
    )(q2, k2, v2)


def _ordered_after(x, anchor):
    rows, width = anchor.shape
    parts = [anchor[r:r + BF16_ROWS, c:c + LANES]
             for r in range(0, rows, BF16_ROWS) for c in range(0, width, LANES)]
    while len(parts) > 1:
        parts = [jnp.maximum(parts[n], parts[n + 1]) for n in range(0, len(parts), 2)]
    a = parts[0]
    head = x[:BF16_ROWS, :LANES] + (a - a)
    top = jnp.concatenate([head, x[:BF16_ROWS, LANES:]], axis=1)
    return jnp.concatenate([top, x[BF16_ROWS:]], axis=0)


def _tail_kernel(tiles_per_seq, u_ref, halo_ref, x_ref, a_ref, p_ref, cw_ref, cb_ref, lg_ref, lb_ref,
                 wout_ref, gf_ref, wup_ref, wdn_ref, gp_ref, wg_ref, wp_ref, gfin_ref,
                 o_ref, conv_ref):
    i = pl.program_id(0)
    n_tiles = pl.num_programs(0) - 1

    @pl.when(i == 0)
    def _():
        conv_ref[...] = jnp.zeros_like(conv_ref)

    tile = jnp.minimum(i, n_tiles - 1)
    starts_seq = lax.rem(tile, tiles_per_seq) == 0
    halo = jnp.where(starts_seq, 0.0, halo_ref[...])

    def conv_chunk(c, not_before):
        def window(lanes):
            if c == 0:
                return jnp.concatenate([halo[:, lanes], u_ref[0:CONV_ROWS, lanes]], axis=0)
            return u_ref[c * CONV_ROWS - CONV_PAD:(c + 1) * CONV_ROWS, lanes]
        y = _conv_rows(window, cw_ref, cb_ref, lg_ref, lb_ref, not_before)
        conv_ref[c * CONV_ROWS:(c + 1) * CONV_ROWS, :] = y
        return y

    assert u_ref.shape[0] == 4 * CONV_ROWS
    h = (x_ref[...]
         + _dot(conv_ref[...], wout_ref[:CONV_CH, :])
         + _dot(a_ref[...], wout_ref[CONV_CH:, :]))
    hn = _rmsnorm(h, gf_ref[...]).astype(BF16)
    gt = _dot(hn, wup_ref[:, :D_FF])
    hn = _ordered_after(_ordered_after(hn, conv_chunk(0, hn)), conv_chunk(1, hn))
    up = _dot(hn, wup_ref[:, D_FF:])
    c2 = conv_chunk(2, hn)
    act = (jax.nn.silu(gt) * up).astype(BF16)
    act = _ordered_after(act, c2)
    h = h + _dot(act, wdn_ref[...])
    hn = _ordered_after(_rmsnorm(h, gp_ref[...]).astype(BF16), conv_chunk(3, act))
    gate = jax.nn.sigmoid(_dot(hn, wg_ref[...]))
    h = h + gate * _dot(p_ref[...].astype(BF16), wp_ref[...])
    o_ref[...] = _rmsnorm(h, gfin_ref[...])


def _tail(u2, x2, attn2, p2, seq, cw, cb, lg, lb, wout, gf, wup, wdn, gp, wg, wp, gfin):
    t = x2.shape[0]
    rows = TAIL_ROWS
    n_tiles = t // rows
    halo_per_tile = rows // CONV_PAD
    cur = lambda i: jnp.minimum(i, n_tiles - 1)
    prev = lambda i: jnp.maximum(i - 1, 0)
    prev_spec = lambda width: pl.BlockSpec((rows, width), lambda i: (prev(i), 0))
    full = lambda a: pl.BlockSpec(a.shape, lambda i: (0, 0), pipeline_mode=pl.Buffered(1))
    return pl.pallas_call(
        functools.partial(_tail_kernel, seq // rows),
        grid=(n_tiles + 1,),
        in_specs=[pl.BlockSpec((rows, CONV_CH), lambda i: (cur(i), 0)),
                  pl.BlockSpec((CONV_PAD, CONV_CH),
                               lambda i: (jnp.maximum(cur(i) * halo_per_tile - 1, 0), 0)),
                  prev_spec(D_MODEL), prev_spec(ATTN_WIDTH), prev_spec(PLE_DIM),
                  full(cw), full(cb), full(lg), full(lb),
                  full(wout), full(gf), full(wup), full(wdn), full(gp), full(wg), full(wp),
                  full(gfin)],
        out_specs=prev_spec(D_MODEL),
        out_shape=jax.ShapeDtypeStruct((t, D_MODEL), F32),
        scratch_shapes=[pltpu.VMEM((rows, CONV_CH), BF16)],
        compiler_params=pltpu.CompilerParams(
            dimension_semantics=("arbitrary",), vmem_limit_bytes=VMEM_LIMIT),
        name="tail",
    )(u2, u2, x2, attn2, p2, cw, cb, lg, lb, wout, gf, wup, wdn, gp, wg, wp, gfin)


def _rope_lane_freq():
    inv_freq = ROPE_THETA ** (-jnp.arange(0, ROPE_DIM, 2, dtype=F32) / ROPE_DIM)
    d = jnp.arange(LANES) % HEAD_DIM
    return jnp.where(d < ROPE_DIM, inv_freq[d % ROPE_HALF], 0.0).reshape(1, LANES).astype(F32)


def kernel(x, p, positions, norm_mix_g, w_in, conv_w, conv_b, conv_ln_g, conv_ln_b, w_out,
           norm_ffn_g, w_ffn_up, w_ffn_down, norm_ple_g, w_ple_gate, w_ple_proj, final_norm_g):
    bsz, seq, _ = x.shape
    t = bsz * seq
    depth = w_in.shape[0]
    row = lambda a: a.reshape(1, -1)
    freq = _rope_lane_freq()
    pos2 = positions.reshape(t, 1)
    h = x.reshape(t, D_MODEL)
    for i in range(depth):
        u, q, k, v = _in_proj(h, pos2, freq, row(norm_mix_g[i]), w_in[i].astype(BF16))
        attn = _attn(q, k, v, bsz, seq)
        assert depth == 1
        h = _tail(u, h, attn, p[i].reshape(t, PLE_DIM), seq, conv_w[i], row(conv_b[i]),
                  row(conv_ln_g[i]), row(conv_ln_b[i]), w_out[i].astype(BF16),
                  row(norm_ffn_g[i]), w_ffn_up[i].astype(BF16), w_ffn_down[i].astype(BF16),
                  row(norm_ple_g[i]), w_ple_gate[i].astype(BF16), w_ple_proj[i].astype(BF16),
                  row(final_norm_g))
    return h.reshape(bsz, seq, D_MODEL)
```

```python
import functools

import jax
import jax.numpy as jnp
from jax import lax
from jax.experimental import pallas as pl
from jax.experimental.pallas import tpu as pltpu

D_MODEL = 1024
CONV_CH = 512
ATTN_HEADS = 8
HEAD_DIM = 64
ATTN_WIDTH = ATTN_HEADS * HEAD_DIM
IN_WIDTH = 2 * CONV_CH + 3 * ATTN_WIDTH
CONV_KERNEL = 31
MOBA_BLOCK = 256
MOBA_TOPK = 3
ROPE_THETA = 500000.0
ROPE_DIM = HEAD_DIM // 4
ROPE_HALF = ROPE_DIM // 2
D_FF = -(-8 * D_MODEL // (3 * 256)) * 256
PLE_DIM = 256
EPS = 1e-6

LANES = 128
SUBLANES = 8
BF16_ROWS = 16
LOG2E = 1.4426950408889634
HEADS_PER_STEP = LANES // HEAD_DIM
CONV_PAD = 32
CONV_ROWS = 64
IN_ROWS = 512
TAIL_ROWS = 256
VMEM_LIMIT = 56 * 1024 * 1024

F32 = jnp.float32
BF16 = jnp.bfloat16
NEG = float(jnp.finfo(jnp.float32).min)
POS = float(jnp.finfo(jnp.float32).max)


def _rmsnorm(x, g):
    return x * lax.rsqrt(jnp.mean(x * x, axis=-1, keepdims=True) + EPS) * g


def _dot(a, b):
    return jnp.dot(a, b, preferred_element_type=F32)


def _dot_nt(a, b):
    return lax.dot_general(a, b, (((1,), (1,)), ((), ())), preferred_element_type=F32)


def _rope_table_kernel(freq_ref, pos_ref, o_ref):
    pos = pos_ref[...].astype(F32)
    for k in range(ROPE_HALF):
        ang = pos * freq_ref[k]
        o_ref[k] = jnp.cos(ang)
        o_ref[ROPE_HALF + k] = jnp.sin(ang)


def _rope_tables(positions, inv_freq):
    t = positions.size
    pos = positions.reshape(t // LANES, LANES)
    table = pl.pallas_call(
        _rope_table_kernel,
        in_specs=[pl.BlockSpec(memory_space=pltpu.SMEM), pl.BlockSpec(memory_space=pltpu.VMEM)],
        out_specs=pl.BlockSpec(memory_space=pltpu.VMEM),
        out_shape=jax.ShapeDtypeStruct((2 * ROPE_HALF, t // LANES, LANES), F32),
        name="rope_table",
    )(inv_freq, pos)
    by_token = table.reshape(2 * ROPE_HALF, t).T
    d = jnp.arange(LANES) % HEAD_DIM
    rotary = d < ROPE_DIM
    cos = jnp.where(rotary, by_token[:, d % ROPE_HALF], 1.0)
    sin = jnp.where(rotary, by_token[:, ROPE_HALF + d % ROPE_HALF], 0.0)
    return cos, sin


def _in_proj_kernel(x_ref, cos_ref, sin_ref, g_ref, w_ref, u_ref, q_ref, k_ref, v_ref):
    hn = _rmsnorm(x_ref[...], g_ref[...]).astype(BF16)
    z = _dot(hn, w_ref[...])
    u_ref[...] = z[:, :CONV_CH] * jax.nn.sigmoid(z[:, CONV_CH:2 * CONV_CH])

    cos = cos_ref[...]
    sin = sin_ref[...]
    d = lax.broadcasted_iota(jnp.int32, (1, LANES), 1) & (HEAD_DIM - 1)
    sin_from_hi = jnp.where(d < ROPE_HALF, -sin, 0.0)
    sin_from_lo = jnp.where(d >= ROPE_HALF, sin, 0.0)

    def rope(t):
        return (t * cos
                + pltpu.roll(t, ROPE_HALF, 1) * sin_from_lo
                + pltpu.roll(t, LANES - ROPE_HALF, 1) * sin_from_hi)

    q0 = 2 * CONV_CH
    k0 = q0 + ATTN_WIDTH
    v0 = k0 + ATTN_WIDTH
    scale = HEAD_DIM ** -0.5 * LOG2E
    for c in range(ATTN_WIDTH // LANES):
        sl = slice(c * LANES, (c + 1) * LANES)
        q_ref[:, sl] = (rope(z[:, q0 + c * LANES:q0 + (c + 1) * LANES]) * scale).astype(BF16)
        k_ref[:, sl] = rope(z[:, k0 + c * LANES:k0 + (c + 1) * LANES]).astype(BF16)
    v_ref[...] = z[:, v0:v0 + ATTN_WIDTH].astype(BF16)


def _in_proj(x2, cos, sin, g, w):
    t = x2.shape[0]
    rows = IN_ROWS
    row_spec = lambda width: pl.BlockSpec((rows, width), lambda i: (i, 0))
    full = lambda a: pl.BlockSpec(a.shape, lambda i: (0, 0))
    return pl.pallas_call(
        _in_proj_kernel,
        grid=(t // rows,),
        in_specs=[row_spec(D_MODEL), row_spec(LANES), row_spec(LANES), full(g), full(w)],
        out_specs=[row_spec(CONV_CH), row_spec(ATTN_WIDTH), row_spec(ATTN_WIDTH), row_spec(ATTN_WIDTH)],
        out_shape=[jax.ShapeDtypeStruct((t, CONV_CH), F32),
                   jax.ShapeDtypeStruct((t, ATTN_WIDTH), BF16),
                   jax.ShapeDtypeStruct((t, ATTN_WIDTH), BF16),
                   jax.ShapeDtypeStruct((t, ATTN_WIDTH), BF16)],
        compiler_params=pltpu.CompilerParams(
            dimension_semantics=("arbitrary",), vmem_limit_bytes=VMEM_LIMIT),
        name="in_proj",
    )(x2, cos, sin, g, w)


def _conv_rows(window, w_ref, b_ref, lg_ref, lb_ref, not_before):
    first = CONV_PAD - (CONV_KERNEL - 1)
    win_rows = CONV_ROWS + CONV_PAD
    a = not_before[:BF16_ROWS, :LANES].astype(F32)[0:1, :]
    hold = a - a
    accs = []
    for g in range(CONV_CH // LANES):
        lanes = slice(g * LANES, (g + 1) * LANES)
        win = window(lanes)
        acc = jnp.broadcast_to(b_ref[:, lanes] + hold, (CONV_ROWS, LANES))
        for phase in range(SUBLANES):
            shifted = win if phase == 0 else pltpu.roll(win, win_rows - phase, 0)
            for base in range(0, CONV_PAD + 1, SUBLANES):
                j = base + phase - first
                if 0 <= j < CONV_KERNEL:
                    acc = acc + w_ref[j:j + 1, lanes] * shifted[base:base + CONV_ROWS]
        accs.append(acc)
        hold = acc[0:1, :] - acc[0:1, :]
    acc = jnp.concatenate(accs, axis=1)
    mu = jnp.mean(acc, axis=-1, keepdims=True)
    dev = acc - mu
    var = jnp.mean(dev * dev, axis=-1, keepdims=True)
    y = dev * lax.rsqrt(var + EPS) * lg_ref[...] + lb_ref[...]
    return (y * jax.nn.sigmoid(y)).astype(BF16)


def _attn_kernel(q_ref, k_ref, v_ref, o_ref, ot_ref):
    seq = q_ref.shape[0]
    nb = seq // MOBA_BLOCK
    q = q_ref[...]
    k = k_ref[...]
    vt = v_ref[...].astype(F32).T.astype(BF16)
    lane = lax.broadcasted_iota(jnp.int32, (1, LANES), 1)
    kmean = jnp.mean(k.astype(F32).reshape(nb, MOBA_BLOCK, LANES), axis=1)
    key_row = lax.broadcasted_iota(jnp.int32, (MOBA_BLOCK, MOBA_BLOCK), 0)
    q_col = lax.broadcasted_iota(jnp.int32, (MOBA_BLOCK, MOBA_BLOCK), 1)
    causal = key_row <= q_col
    blk = lax.broadcasted_iota(jnp.int32, (nb, MOBA_BLOCK), 0)
    first_ranked = MOBA_TOPK + 1
    ones = jnp.ones((BF16_ROWS, seq), BF16)

    kh, vth, gate_t = [], [], []
    for h in range(HEADS_PER_STEP):
        in_head = (lane >= h * HEAD_DIM) & (lane < (h + 1) * HEAD_DIM)
        kh.append(jnp.where(in_head, k, jnp.zeros_like(k)))
        vth.append(jnp.concatenate([vt[h * HEAD_DIM:(h + 1) * HEAD_DIM], ones], axis=0))
        kmh = jnp.where(in_head, kmean, 0.0).astype(BF16)
        gate_t.append(_dot_nt(kmh, q[first_ranked * MOBA_BLOCK:]))

    def scores(h, i):
        qi = q[i * MOBA_BLOCK:(i + 1) * MOBA_BLOCK]
        return _dot_nt(kh[h][:(i + 1) * MOBA_BLOCK], qi)

    def finish(h, i, st):
        if i >= first_ranked:
            c0 = (i - first_ranked) * MOBA_BLOCK
            gi = gate_t[h][:, c0:c0 + MOBA_BLOCK]
            rank = jnp.zeros((nb, MOBA_BLOCK), jnp.int32)
            for m in range(i):
                gm = gi[m:m + 1, :]
                beats = (gm > gi) | ((gm == gi) & (blk > m))
                rank = rank + jnp.where(beats, 1, 0)
            cap = jnp.where(rank < MOBA_TOPK, POS, NEG)
        pieces = []
        for j in range(i + 1):
            sj = st[j * MOBA_BLOCK:(j + 1) * MOBA_BLOCK]
            if j == i:
                sj = jnp.where(causal, sj, NEG)
            elif i >= first_ranked:
                sj = jnp.minimum(sj, cap[j:j + 1, :])
            pieces.append(sj)
        st = pieces[0] if i == 0 else jnp.concatenate(pieces, axis=0)
        m_col = jnp.max(st, axis=0, keepdims=True)
        p = jnp.exp2(st - m_col).astype(BF16)
        ot = _dot(vth[h][:, :(i + 1) * MOBA_BLOCK], p)
        ot_ref[h * HEAD_DIM:(h + 1) * HEAD_DIM, i * MOBA_BLOCK:(i + 1) * MOBA_BLOCK] = (
            ot[:HEAD_DIM] / ot[HEAD_DIM:HEAD_DIM + 1])

    units = [(h, i) for i in range(nb) for h in range(HEADS_PER_STEP)]
    st_next = scores(*units[0])
    for n, (h, i) in enumerate(units):
        st = st_next
        if n + 1 < len(units):
            st_next = scores(*units[n + 1])
        finish(h, i, st)
    o_ref[...] = ot_ref[...].T.astype(BF16)


def _attn(q2, k2, v2, bsz, seq):
    spec = pl.BlockSpec((seq, LANES), lambda b, hp: (b, hp))
    return pl.pallas_call(
        _attn_kernel,
        grid=(bsz, ATTN_WIDTH // LANES),
        in_specs=[spec, spec, spec],
        out_specs=spec,
        out_shape=jax.ShapeDtypeStruct(q2.shape, BF16),
        scratch_shapes=[pltpu.VMEM((LANES, seq), F32)],
        compiler_params=pltpu.CompilerParams(
            dimension_semantics=("arbitrary", "arbitrary"), vmem_limit_bytes=VMEM_LIMIT),
        name="attn",
    )(q2, k2, v2)


def _ordered_after(x, anchor):
    rows, width = anchor.shape
    parts = [anchor[r:r + BF16_ROWS, c:c + LANES]
             for r in range(0, rows, BF16_ROWS) for c in range(0, width, LANES)]
    while len(parts) > 1:
        parts = [jnp.maximum(parts[n], parts[n + 1]) for n in range(0, len(parts), 2)]
    a = parts[0]
    head = x[:BF16_ROWS, :LANES] + (a - a)
    top = jnp.concatenate([head, x[:BF16_ROWS, LANES:]], axis=1)
    return jnp.concatenate([top, x[BF16_ROWS:]], axis=0)


def _tail_kernel(tiles_per_seq, u_ref, halo_ref, x_ref, a_ref, p_ref, cw_ref, cb_ref, lg_ref, lb_ref,
                 wout_ref, gf_ref, wup_ref, wdn_ref, gp_ref, wg_ref, wp_ref, gfin_ref,
                 o_ref, conv_ref):
    i = pl.program_id(0)
    n_tiles = pl.num_programs(0) - 1

    @pl.when(i == 0)
    def _():
        conv_ref[...] = jnp.zeros_like(conv_ref)

    tile = jnp.minimum(i, n_tiles - 1)
    starts_seq = lax.rem(tile, tiles_per_seq) == 0
    halo = jnp.where(starts_seq, 0.0, halo_ref[...])

    def conv_chunk(c, not_before):
        def window(lanes):
            if c == 0:
                return jnp.concatenate([halo[:, lanes], u_ref[0:CONV_ROWS, lanes]], axis=0)
            return u_ref[c * CONV_ROWS - CONV_PAD:(c + 1) * CONV_ROWS, lanes]
        y = _conv_rows(window, cw_ref, cb_ref, lg_ref, lb_ref, not_before)
        conv_ref[c * CONV_ROWS:(c + 1) * CONV_ROWS, :] = y
        return y

    h = (x_ref[...]
         + _dot(conv_ref[...], wout_ref[:CONV_CH, :])
         + _dot(a_ref[...], wout_ref[CONV_CH:, :]))
    hn = _rmsnorm(h, gf_ref[...]).astype(BF16)
    assert u_ref.shape[0] == 4 * CONV_ROWS
    gt = _dot(hn, wup_ref[:, :D_FF])
    hn = _ordered_after(_ordered_after(hn, conv_chunk(0, hn)), conv_chunk(1, hn))
    up = _dot(hn, wup_ref[:, D_FF:])
    c2 = conv_chunk(2, hn)
    act = (jax.nn.silu(gt) * up).astype(BF16)
    act = _ordered_after(act, c2)
    h = h + _dot(act, wdn_ref[...])
    hn = _ordered_after(_rmsnorm(h, gp_ref[...]).astype(BF16), conv_chunk(3, act))
    gate = jax.nn.sigmoid(_dot(hn, wg_ref[...]))
    h = h + gate * _dot(p_ref[...].astype(BF16), wp_ref[...])
    o_ref[...] = _rmsnorm(h, gfin_ref[...])


def _tail(u2, x2, attn2, p2, seq, cw, cb, lg, lb, wout, gf, wup, wdn, gp, wg, wp, gfin):
    t = x2.shape[0]
    rows = TAIL_ROWS
    n_tiles = t // rows
    halo_per_tile = rows // CONV_PAD
    cur = lambda i: jnp.minimum(i, n_tiles - 1)
    prev = lambda i: jnp.maximum(i - 1, 0)
    prev_spec = lambda width: pl.BlockSpec((rows, width), lambda i: (prev(i), 0))
    full = lambda a: pl.BlockSpec(a.shape, lambda i: (0, 0), pipeline_mode=pl.Buffered(1))
    return pl.pallas_call(
        functools.partial(_tail_kernel, seq // rows),
        grid=(n_tiles + 1,),
        in_specs=[pl.BlockSpec((rows, CONV_CH), lambda i: (cur(i), 0)),
                  pl.BlockSpec((CONV_PAD, CONV_CH),
                               lambda i: (jnp.maximum(cur(i) * halo_per_tile - 1, 0), 0)),
                  prev_spec(D_MODEL), prev_spec(ATTN_WIDTH), prev_spec(PLE_DIM),
                  full(cw), full(cb), full(lg), full(lb),
                  full(wout), full(gf), full(wup), full(wdn), full(gp), full(wg), full(wp),
                  full(gfin)],
        out_specs=prev_spec(D_MODEL),
        out_shape=jax.ShapeDtypeStruct((t, D_MODEL), F32),
        scratch_shapes=[pltpu.VMEM((rows, CONV_CH), BF16)],
        compiler_params=pltpu.CompilerParams(
            dimension_semantics=("arbitrary",), vmem_limit_bytes=VMEM_LIMIT),
        name="tail",
    )(u2, u2, x2, attn2, p2, cw, cb, lg, lb, wout, gf, wup, wdn, gp, wg, wp, gfin)


def kernel(x, p, positions, norm_mix_g, w_in, conv_w, conv_b, conv_ln_g, conv_ln_b, w_out,
           norm_ffn_g, w_ffn_up, w_ffn_down, norm_ple_g, w_ple_gate, w_ple_proj, final_norm_g):
    bsz, seq, _ = x.shape
    t = bsz * seq
    depth = w_in.shape[0]
    row = lambda a: a.reshape(1, -1)
    inv_freq = ROPE_THETA ** (-jnp.arange(0, ROPE_DIM, 2, dtype=F32) / ROPE_DIM)
    cos, sin = _rope_tables(positions, inv_freq)
    h = x.reshape(t, D_MODEL)
    for i in range(depth):
        u, q, k, v = _in_proj(h, cos, sin, row(norm_mix_g[i]), w_in[i].astype(BF16))
        attn = _attn(q, k, v, bsz, seq)
        assert depth == 1
        h = _tail(u, h, attn, p[i].reshape(t, PLE_DIM), seq, conv_w[i], row(conv_b[i]),
                  row(conv_ln_g[i]), row(conv_ln_b[i]), w_out[i].astype(BF16),
                  row(norm_ffn_g[i]), w_ffn_up[i].astype(BF16), w_ffn_down[i].astype(BF16),
                  row(norm_ple_g[i]), w_ple_gate[i].astype(BF16), w_ple_proj[i].astype(BF16),
                  row(final_norm_g))
    return h.reshape(bsz, seq, D_MODEL)
```

```python
import functools

import jax
import jax.numpy as jnp
from jax import lax
from jax.experimental import pallas as pl
from jax.experimental.pallas import tpu as pltpu

D_MODEL = 1024
CONV_CH = 512
ATTN_HEADS = 8
HEAD_DIM = 64
ATTN_WIDTH = ATTN_HEADS * HEAD_DIM
IN_WIDTH = 2 * CONV_CH + 3 * ATTN_WIDTH
CONV_KERNEL = 31
MOBA_BLOCK = 256
MOBA_TOPK = 3
ROPE_THETA = 500000.0
ROPE_DIM = HEAD_DIM // 4
ROPE_HALF = ROPE_DIM // 2
D_FF = -(-8 * D_MODEL // (3 * 256)) * 256
PLE_DIM = 256
EPS = 1e-6

LANES = 128
SUBLANES = 8
BF16_ROWS = 16
LOG2E = 1.4426950408889634
HEADS_PER_STEP = LANES // HEAD_DIM
CONV_PAD = 32
CONV_ROWS = 64
IN_ROWS = 512
TAIL_ROWS = 256
VMEM_LIMIT = 56 * 1024 * 1024

F32 = jnp.float32
BF16 = jnp.bfloat16
NEG = float(jnp.finfo(jnp.float32).min)
POS = float(jnp.finfo(jnp.float32).max)


def _rmsnorm(x, g):
    return x * lax.rsqrt(jnp.mean(x * x, axis=-1, keepdims=True) + EPS) * g


def _dot(a, b):
    return jnp.dot(a, b, preferred_element_type=F32)


def _dot_nt(a, b):
    return lax.dot_general(a, b, (((1,), (1,)), ((), ())), preferred_element_type=F32)


def _rope_expand_matrix():
    r = jnp.arange(LANES)[:, None]
    l = jnp.arange(2 * LANES)[None, :]
    k = r % (2 * ROPE_HALF)
    piece_ok = r < 3 * 2 * ROPE_HALF
    d = (l % LANES) % HEAD_DIM
    want = d % ROPE_HALF + ROPE_HALF * (l // LANES)
    return (piece_ok & (d < ROPE_DIM) & (k == want)).astype(BF16)


def _in_proj_kernel(x_ref, pos_ref, freq_ref, expand_ref, g_ref, w_ref, u_ref, q_ref, k_ref, v_ref):
    hn = _rmsnorm(x_ref[...], g_ref[...]).astype(BF16)
    z = _dot(hn, w_ref[...])
    u_ref[...] = z[:, :CONV_CH] * jax.nn.sigmoid(z[:, CONV_CH:2 * CONV_CH])

    ang = pos_ref[0].astype(F32) * freq_ref[...]
    small = jnp.concatenate([jnp.cos(ang), jnp.sin(ang)], axis=0)
    hi = small.astype(BF16).astype(F32)
    mid = (small - hi).astype(BF16).astype(F32)
    lo = (small - hi - mid).astype(BF16).astype(F32)
    pad = jnp.zeros((LANES - 3 * small.shape[0], small.shape[1]), F32)
    pieces = jnp.concatenate([hi, mid, lo, pad], axis=0).T.astype(BF16)
    table = _dot(pieces, expand_ref[...])
    d = lax.broadcasted_iota(jnp.int32, (1, LANES), 1) & (HEAD_DIM - 1)
    cos = table[:, :LANES] + jnp.where(d < ROPE_DIM, 0.0, 1.0)
    sin = table[:, LANES:]
    sin_from_hi = jnp.where(d < ROPE_HALF, -sin, 0.0)
    sin_from_lo = jnp.where(d >= ROPE_HALF, sin, 0.0)

    def rope(t):
        return (t * cos
                + pltpu.roll(t, ROPE_HALF, 1) * sin_from_lo
                + pltpu.roll(t, LANES - ROPE_HALF, 1) * sin_from_hi)

    q0 = 2 * CONV_CH
    k0 = q0 + ATTN_WIDTH
    v0 = k0 + ATTN_WIDTH
    scale = HEAD_DIM ** -0.5 * LOG2E
    for c in range(ATTN_WIDTH // LANES):
        sl = slice(c * LANES, (c + 1) * LANES)
        q_ref[:, sl] = (rope(z[:, q0 + c * LANES:q0 + (c + 1) * LANES]) * scale).astype(BF16)
        k_ref[:, sl] = rope(z[:, k0 + c * LANES:k0 + (c + 1) * LANES]).astype(BF16)
    v_ref[...] = z[:, v0:v0 + ATTN_WIDTH].astype(BF16)


def _in_proj(x2, positions, inv_freq, g, w):
    t = x2.shape[0]
    rows = IN_ROWS
    pos3 = positions.reshape(t // rows, 1, rows)
    freq = jnp.broadcast_to(inv_freq[:, None], (ROPE_HALF, rows))
    expand = _rope_expand_matrix()
    row_spec = lambda width: pl.BlockSpec((rows, width), lambda i: (i, 0))
    full = lambda a: pl.BlockSpec(a.shape, lambda i: (0, 0))
    return pl.pallas_call(
        _in_proj_kernel,
        grid=(t // rows,),
        in_specs=[row_spec(D_MODEL), pl.BlockSpec((1, 1, rows), lambda i: (i, 0, 0)),
                  full(freq), full(expand), full(g), full(w)],
        out_specs=[row_spec(CONV_CH), row_spec(ATTN_WIDTH), row_spec(ATTN_WIDTH), row_spec(ATTN_WIDTH)],
        out_shape=[jax.ShapeDtypeStruct((t, CONV_CH), F32),
                   jax.ShapeDtypeStruct((t, ATTN_WIDTH), BF16),
                   jax.ShapeDtypeStruct((t, ATTN_WIDTH), BF16),
                   jax.ShapeDtypeStruct((t, ATTN_WIDTH), BF16)],
        compiler_params=pltpu.CompilerParams(
            dimension_semantics=("arbitrary",), vmem_limit_bytes=VMEM_LIMIT),
        name="in_proj",
    )(x2, pos3, freq, expand, g, w)


def _conv_rows(window, w_ref, b_ref, lg_ref, lb_ref, not_before):
    first = CONV_PAD - (CONV_KERNEL - 1)
    win_rows = CONV_ROWS + CONV_PAD
    a = not_before[:BF16_ROWS, :LANES].astype(F32)[0:1, :]
    hold = a - a
    accs = []
    for g in range(CONV_CH // LANES):
        lanes = slice(g * LANES, (g + 1) * LANES)
        win = window(lanes)
        acc = jnp.broadcast_to(b_ref[:, lanes] + hold, (CONV_ROWS, LANES))
        for phase in range(SUBLANES):
            shifted = win if phase == 0 else pltpu.roll(win, win_rows - phase, 0)
            for base in range(0, CONV_PAD + 1, SUBLANES):
                j = base + phase - first
                if 0 <= j < CONV_KERNEL:
                    acc = acc + w_ref[j:j + 1, lanes] * shifted[base:base + CONV_ROWS]
        accs.append(acc)
        hold = acc[0:1, :] - acc[0:1, :]
    acc = jnp.concatenate(accs, axis=1)
    mu = jnp.mean(acc, axis=-1, keepdims=True)
    dev = acc - mu
    var = jnp.mean(dev * dev, axis=-1, keepdims=True)
    y = dev * lax.rsqrt(var + EPS) * lg_ref[...] + lb_ref[...]
    return (y * jax.nn.sigmoid(y)).astype(BF16)


def _attn_kernel(q_ref, k_ref, v_ref, o_ref, ot_ref):
    seq = q_ref.shape[0]
    nb = seq // MOBA_BLOCK
    q = q_ref[...]
    k = k_ref[...]
    vt = v_ref[...].astype(F32).T.astype(BF16)
    lane = lax.broadcasted_iota(jnp.int32, (1, LANES), 1)
    kmean = jnp.mean(k.astype(F32).reshape(nb, MOBA_BLOCK, LANES), axis=1)
    key_row = lax.broadcasted_iota(jnp.int32, (MOBA_BLOCK, MOBA_BLOCK), 0)
    q_col = lax.broadcasted_iota(jnp.int32, (MOBA_BLOCK, MOBA_BLOCK), 1)
    causal = key_row <= q_col
    blk = lax.broadcasted_iota(jnp.int32, (nb, MOBA_BLOCK), 0)
    first_ranked = MOBA_TOPK + 1
    ones = jnp.ones((BF16_ROWS, seq), BF16)

    kh, vth, gate_t = [], [], []
    for h in range(HEADS_PER_STEP):
        in_head = (lane >= h * HEAD_DIM) & (lane < (h + 1) * HEAD_DIM)
        kh.append(jnp.where(in_head, k, jnp.zeros_like(k)))
        vth.append(jnp.concatenate([vt[h * HEAD_DIM:(h + 1) * HEAD_DIM], ones], axis=0))
        kmh = jnp.where(in_head, kmean, 0.0).astype(BF16)
        gate_t.append(_dot_nt(kmh, q[first_ranked * MOBA_BLOCK:]))

    def scores(h, i):
        qi = q[i * MOBA_BLOCK:(i + 1) * MOBA_BLOCK]
        return _dot_nt(kh[h][:(i + 1) * MOBA_BLOCK], qi)

    def finish(h, i, st):
        if i >= first_ranked:
            c0 = (i - first_ranked) * MOBA_BLOCK
            gi = gate_t[h][:, c0:c0 + MOBA_BLOCK]
            rank = jnp.zeros((nb, MOBA_BLOCK), jnp.int32)
            for m in range(i):
                gm = gi[m:m + 1, :]
                beats = (gm > gi) | ((gm == gi) & (blk > m))
                rank = rank + jnp.where(beats, 1, 0)
            cap = jnp.where(rank < MOBA_TOPK, POS, NEG)
        pieces = []
        for j in range(i + 1):
            sj = st[j * MOBA_BLOCK:(j + 1) * MOBA_BLOCK]
            if j == i:
                sj = jnp.where(causal, sj, NEG)
            elif i >= first_ranked:
                sj = jnp.minimum(sj, cap[j:j + 1, :])
            pieces.append(sj)
        st = pieces[0] if i == 0 else jnp.concatenate(pieces, axis=0)
        m_col = jnp.max(st, axis=0, keepdims=True)
        p = jnp.exp2(st - m_col).astype(BF16)
        ot = _dot(vth[h][:, :(i + 1) * MOBA_BLOCK], p)
        ot_ref[h * HEAD_DIM:(h + 1) * HEAD_DIM, i * MOBA_BLOCK:(i + 1) * MOBA_BLOCK] = (
            ot[:HEAD_DIM] / ot[HEAD_DIM:HEAD_DIM + 1])

    units = [(h, i) for i in range(nb) for h in range(HEADS_PER_STEP)]
    st_next = scores(*units[0])
    for n, (h, i) in enumerate(units):
        st = st_next
        if n + 1 < len(units):
            st_next = scores(*units[n + 1])
        finish(h, i, st)
    o_ref[...] = ot_ref[...].T.astype(BF16)


def _attn(q2, k2, v2, bsz, seq):
    spec = pl.BlockSpec((seq, LANES), lambda b, hp: (b, hp))
    return pl.pallas_call(
        _attn_kernel,
        grid=(bsz, ATTN_WIDTH // LANES),
        in_specs=[spec, spec, spec],
        out_specs=spec,
        out_shape=jax.ShapeDtypeStruct(q2.shape, BF16),
        scratch_shapes=[pltpu.VMEM((LANES, seq), F32)],
        compiler_params=pltpu.CompilerParams(
            dimension_semantics=("arbitrary", "arbitrary"), vmem_limit_bytes=VMEM_LIMIT),
        name="attn",
    )(q2, k2, v2)


def _ordered_after(x, anchor):
    rows, width = anchor.shape
    parts = [anchor[r:r + BF16_ROWS, c:c + LANES]
             for r in range(0, rows, BF16_ROWS) for c in range(0, width, LANES)]
    while len(parts) > 1:
        parts = [jnp.maximum(parts[n], parts[n + 1]) for n in range(0, len(parts), 2)]
    a = parts[0]
    head = x[:BF16_ROWS, :LANES] + (a - a)
    top = jnp.concatenate([head, x[:BF16_ROWS, LANES:]], axis=1)
    return jnp.concatenate([top, x[BF16_ROWS:]], axis=0)


def _tail_kernel(tiles_per_seq, u_ref, halo_ref, x_ref, a_ref, p_ref, cw_ref, cb_ref, lg_ref, lb_ref,
                 wout_ref, gf_ref, wup_ref, wdn_ref, gp_ref, wg_ref, wp_ref, gfin_ref,
                 o_ref, conv_ref):
    i = pl.program_id(0)
    n_tiles = pl.num_programs(0) - 1

    @pl.when(i == 0)
    def _():
        conv_ref[...] = jnp.zeros_like(conv_ref)

    tile = jnp.minimum(i, n_tiles - 1)
    starts_seq = lax.rem(tile, tiles_per_seq) == 0
    halo = jnp.where(starts_seq, 0.0, halo_ref[...])

    def conv_chunk(c, not_before):
        def window(lanes):
            if c == 0:
                return jnp.concatenate([halo[:, lanes], u_ref[0:CONV_ROWS, lanes]], axis=0)
            return u_ref[c * CONV_ROWS - CONV_PAD:(c + 1) * CONV_ROWS, lanes]
        y = _conv_rows(window, cw_ref, cb_ref, lg_ref, lb_ref, not_before)
        conv_ref[c * CONV_ROWS:(c + 1) * CONV_ROWS, :] = y
        return y

    h = (x_ref[...]
         + _dot(conv_ref[...], wout_ref[:CONV_CH, :])
         + _dot(a_ref[...], wout_ref[CONV_CH:, :]))
    hn = _rmsnorm(h, gf_ref[...]).astype(BF16)
    assert u_ref.shape[0] == 4 * CONV_ROWS
    gt = _dot(hn, wup_ref[:, :D_FF])
    hn = _ordered_after(_ordered_after(hn, conv_chunk(0, hn)), conv_chunk(1, hn))
    up = _dot(hn, wup_ref[:, D_FF:])
    c2 = conv_chunk(2, hn)
    act = (jax.nn.silu(gt) * up).astype(BF16)
    act = _ordered_after(act, c2)
    h = h + _dot(act, wdn_ref[...])
    hn = _ordered_after(_rmsnorm(h, gp_ref[...]).astype(BF16), conv_chunk(3, act))
    gate = jax.nn.sigmoid(_dot(hn, wg_ref[...]))
    h = h + gate * _dot(p_ref[...].astype(BF16), wp_ref[...])
    o_ref[...] = _rmsnorm(h, gfin_ref[...])


def _tail(u2, x2, attn2, p2, seq, cw, cb, lg, lb, wout, gf, wup, wdn, gp, wg, wp, gfin):
    t = x2.shape[0]
    rows = TAIL_ROWS
    n_tiles = t // rows
    halo_per_tile = rows // CONV_PAD
    cur = lambda i: jnp.minimum(i, n_tiles - 1)
    prev = lambda i: jnp.maximum(i - 1, 0)
    prev_spec = lambda width: pl.BlockSpec((rows, width), lambda i: (prev(i), 0))
    full = lambda a: pl.BlockSpec(a.shape, lambda i: (0, 0), pipeline_mode=pl.Buffered(1))
    return pl.pallas_call(
        functools.partial(_tail_kernel, seq // rows),
        grid=(n_tiles + 1,),
        in_specs=[pl.BlockSpec((rows, CONV_CH), lambda i: (cur(i), 0)),
                  pl.BlockSpec((CONV_PAD, CONV_CH),
                               lambda i: (jnp.maximum(cur(i) * halo_per_tile - 1, 0), 0)),
                  prev_spec(D_MODEL), prev_spec(ATTN_WIDTH), prev_spec(PLE_DIM),
                  full(cw), full(cb), full(lg), full(lb),
                  full(wout), full(gf), full(wup), full(wdn), full(gp), full(wg), full(wp),
                  full(gfin)],
        out_specs=prev_spec(D_MODEL),
        out_shape=jax.ShapeDtypeStruct((t, D_MODEL), F32),
        scratch_shapes=[pltpu.VMEM((rows, CONV_CH), BF16)],
        compiler_params=pltpu.CompilerParams(
            dimension_semantics=("arbitrary",), vmem_limit_bytes=VMEM_LIMIT),
        name="tail",
    )(u2, u2, x2, attn2, p2, cw, cb, lg, lb, wout, gf, wup, wdn, gp, wg, wp, gfin)


def kernel(x, p, positions, norm_mix_g, w_in, conv_w, conv_b, conv_ln_g, conv_ln_b, w_out,
           norm_ffn_g, w_ffn_up, w_ffn_down, norm_ple_g, w_ple_gate, w_ple_proj, final_norm_g):
    bsz, seq, _ = x.shape
    t = bsz * seq
    depth = w_in.shape[0]
    row = lambda a: a.reshape(1, -1)
    inv_freq = ROPE_THETA ** (-jnp.arange(0, ROPE_DIM, 2, dtype=F32) / ROPE_DIM)
    h = x.reshape(t, D_MODEL)
    for i in range(depth):
        u, q, k, v = _in_proj(h, positions, inv_freq, row(norm_mix_g[i]), w_in[i].astype(BF16))
        attn = _attn(q, k, v, bsz, seq)
        assert depth == 1
        h = _tail(u, h, attn, p[i].reshape(t, PLE_DIM), seq, conv_w[i], row(conv_b[i]),
                  row(conv_ln_g[i]), row(conv_ln_b[i]), w_out[i].astype(BF16),
                  row(norm_ffn_g[i]), w_ffn_up[i].astype(BF16), w_ffn_down[i].astype(BF16),
                  row(norm_ple_g[i]), w_ple_gate[i].astype(BF16), w_ple_proj[i].astype(BF16),
                  row(final_norm_g))
    return h.reshape(bsz, seq, D_MODEL)
```

```python
import functools

import jax
import jax.numpy as jnp
from jax import lax
from jax.experimental import pallas as pl
from jax.experimental.pallas import tpu as pltpu

D_MODEL = 1024
CONV_CH = 512
ATTN_HEADS = 8
HEAD_DIM = 64
ATTN_WIDTH = ATTN_HEADS * HEAD_DIM
IN_WIDTH = 2 * CONV_CH + 3 * ATTN_WIDTH
CONV_KERNEL = 31
MOBA_BLOCK = 256
MOBA_TOPK = 3
ROPE_THETA = 500000.0
ROPE_DIM = HEAD_DIM // 4
ROPE_HALF = ROPE_DIM // 2
D_FF = -(-8 * D_MODEL // (3 * 256)) * 256
PLE_DIM = 256
EPS = 1e-6

LANES = 128
SUBLANES = 8
BF16_ROWS = 16
LOG2E = 1.4426950408889634
HEADS_PER_STEP = LANES // HEAD_DIM
CONV_PAD = 32
CONV_ROWS = 64
IN_ROWS = 512
TAIL_ROWS = 256
VMEM_LIMIT = 56 * 1024 * 1024

F32 = jnp.float32
BF16 = jnp.bfloat16
NEG = float(jnp.finfo(jnp.float32).min)
POS = float(jnp.finfo(jnp.float32).max)


def _rmsnorm(x, g):
    return x * lax.rsqrt(jnp.mean(x * x, axis=-1, keepdims=True) + EPS) * g


def _dot(a, b):
    return jnp.dot(a, b, preferred_element_type=F32)


def _dot_nt(a, b):
    return lax.dot_general(a, b, (((1,), (1,)), ((), ())), preferred_element_type=F32)


def _rope_expand_matrix():
    r = jnp.arange(LANES)[:, None]
    l = jnp.arange(2 * LANES)[None, :]
    k = r % (2 * ROPE_HALF)
    piece_ok = r < 3 * 2 * ROPE_HALF
    d = (l % LANES) % HEAD_DIM
    want = d % ROPE_HALF + ROPE_HALF * (l // LANES)
    return (piece_ok & (d < ROPE_DIM) & (k == want)).astype(BF16)


def _in_proj_kernel(x_ref, pos_ref, freq_ref, expand_ref, g_ref, w_ref, u_ref, q_ref, k_ref, v_ref):
    rows = x_ref.shape[0]
    half = rows // 2
    zs = []
    for r in range(2):
        hn = _rmsnorm(x_ref[r * half:(r + 1) * half, :], g_ref[...]).astype(BF16)
        zs.append(_dot(hn, w_ref[...]))

    ang = pos_ref[0].astype(F32) * freq_ref[...]
    small = jnp.concatenate([jnp.cos(ang), jnp.sin(ang)], axis=0)
    hi = small.astype(BF16).astype(F32)
    mid = (small - hi).astype(BF16).astype(F32)
    lo = (small - hi - mid).astype(BF16).astype(F32)
    pad = jnp.zeros((LANES - 3 * small.shape[0], small.shape[1]), F32)
    pieces = jnp.concatenate([hi, mid, lo, pad], axis=0).T.astype(BF16)
    table = _dot(pieces, expand_ref[...])
    d = lax.broadcasted_iota(jnp.int32, (1, LANES), 1) & (HEAD_DIM - 1)
    cos = table[:, :LANES] + jnp.where(d < ROPE_DIM, 0.0, 1.0)
    sin = table[:, LANES:]
    sin_from_hi = jnp.where(d < ROPE_HALF, -sin, 0.0)
    sin_from_lo = jnp.where(d >= ROPE_HALF, sin, 0.0)

    q0 = 2 * CONV_CH
    k0 = q0 + ATTN_WIDTH
    v0 = k0 + ATTN_WIDTH
    scale = HEAD_DIM ** -0.5 * LOG2E
    for r, z in enumerate(zs):
        rs = slice(r * half, (r + 1) * half)
        u_ref[rs, :] = z[:, :CONV_CH] * jax.nn.sigmoid(z[:, CONV_CH:2 * CONV_CH])

        def rope(t, rs=rs):
            return (t * cos[rs]
                    + pltpu.roll(t, ROPE_HALF, 1) * sin_from_lo[rs]
                    + pltpu.roll(t, LANES - ROPE_HALF, 1) * sin_from_hi[rs])

        for c in range(ATTN_WIDTH // LANES):
            sl = slice(c * LANES, (c + 1) * LANES)
            q_ref[rs, sl] = (rope(z[:, q0 + c * LANES:q0 + (c + 1) * LANES]) * scale).astype(BF16)
            k_ref[rs, sl] = rope(z[:, k0 + c * LANES:k0 + (c + 1) * LANES]).astype(BF16)
        v_ref[rs, :] = z[:, v0:v0 + ATTN_WIDTH].astype(BF16)


def _in_proj(x2, positions, inv_freq, g, w):
    t = x2.shape[0]
    rows = IN_ROWS
    pos3 = positions.reshape(t // rows, 1, rows)
    freq = jnp.broadcast_to(inv_freq[:, None], (ROPE_HALF, rows))
    expand = _rope_expand_matrix()
    row_spec = lambda width: pl.BlockSpec((rows, width), lambda i: (i, 0))
    full = lambda a: pl.BlockSpec(a.shape, lambda i: (0, 0))
    return pl.pallas_call(
        _in_proj_kernel,
        grid=(t // rows,),
        in_specs=[row_spec(D_MODEL), pl.BlockSpec((1, 1, rows), lambda i: (i, 0, 0)),
                  full(freq), full(expand), full(g), full(w)],
        out_specs=[row_spec(CONV_CH), row_spec(ATTN_WIDTH), row_spec(ATTN_WIDTH), row_spec(ATTN_WIDTH)],
        out_shape=[jax.ShapeDtypeStruct((t, CONV_CH), F32),
                   jax.ShapeDtypeStruct((t, ATTN_WIDTH), BF16),
                   jax.ShapeDtypeStruct((t, ATTN_WIDTH), BF16),
                   jax.ShapeDtypeStruct((t, ATTN_WIDTH), BF16)],
        compiler_params=pltpu.CompilerParams(
            dimension_semantics=("arbitrary",), vmem_limit_bytes=VMEM_LIMIT),
        name="in_proj",
    )(x2, pos3, freq, expand, g, w)


def _conv_rows(window, w_ref, b_ref, lg_ref, lb_ref, not_before):
    first = CONV_PAD - (CONV_KERNEL - 1)
    win_rows = CONV_ROWS + CONV_PAD
    a = not_before[:BF16_ROWS, :LANES].astype(F32)[0:1, :]
    hold = a - a
    accs = []
    for g in range(CONV_CH // LANES):
        lanes = slice(g * LANES, (g + 1) * LANES)
        win = window(lanes)
        acc = jnp.broadcast_to(b_ref[:, lanes] + hold, (CONV_ROWS, LANES))
        for phase in range(SUBLANES):
            shifted = win if phase == 0 else pltpu.roll(win, win_rows - phase, 0)
            for base in range(0, CONV_PAD + 1, SUBLANES):
                j = base + phase - first
                if 0 <= j < CONV_KERNEL:
                    acc = acc + w_ref[j:j + 1, lanes] * shifted[base:base + CONV_ROWS]
        accs.append(acc)
        hold = acc[0:1, :] - acc[0:1, :]
    acc = jnp.concatenate(accs, axis=1)
    mu = jnp.mean(acc, axis=-1, keepdims=True)
    dev = acc - mu
    var = jnp.mean(dev * dev, axis=-1, keepdims=True)
    y = dev * lax.rsqrt(var + EPS) * lg_ref[...] + lb_ref[...]
    return (y * jax.nn.sigmoid(y)).astype(BF16)


def _attn_kernel(q_ref, k_ref, v_ref, o_ref, ot_ref):
    seq = q_ref.shape[0]
    nb = seq // MOBA_BLOCK
    q = q_ref[...]
    k = k_ref[...]
    vt = v_ref[...].astype(F32).T.astype(BF16)
    lane = lax.broadcasted_iota(jnp.int32, (1, LANES), 1)
    kmean = jnp.mean(k.astype(F32).reshape(nb, MOBA_BLOCK, LANES), axis=1)
    heads = HEADS_PER_STEP
    cols = heads * MOBA_BLOCK
    key_row = lax.broadcasted_iota(jnp.int32, (MOBA_BLOCK, cols), 0)
    q_col = lax.broadcasted_iota(jnp.int32, (MOBA_BLOCK, cols), 1) & (MOBA_BLOCK - 1)
    causal = key_row <= q_col
    blk = lax.broadcasted_iota(jnp.int32, (nb, MOBA_BLOCK), 0)
    first_ranked = MOBA_TOPK + 1
    vt_aug = jnp.concatenate([vt, jnp.ones((BF16_ROWS, seq), BF16)], axis=0)

    in_head = [(lane >= h * HEAD_DIM) & (lane < (h + 1) * HEAD_DIM) for h in range(heads)]
    gate_t = [_dot_nt(jnp.where(m, kmean, 0.0).astype(BF16), q[first_ranked * MOBA_BLOCK:])
              for m in in_head]

    def scores(i):
        qi = q[i * MOBA_BLOCK:(i + 1) * MOBA_BLOCK]
        q2 = jnp.concatenate([jnp.where(m, qi, jnp.zeros_like(qi)) for m in in_head], axis=0)
        return _dot_nt(k[:(i + 1) * MOBA_BLOCK], q2)

    def finish(i, st):
        if i >= first_ranked:
            caps = []
            for h in range(heads):
                c0 = (i - first_ranked) * MOBA_BLOCK
                gi = gate_t[h][:, c0:c0 + MOBA_BLOCK]
                rank = jnp.zeros((nb, MOBA_BLOCK), jnp.int32)
                for m in range(i):
                    gm = gi[m:m + 1, :]
                    beats = (gm > gi) | ((gm == gi) & (blk > m))
                    rank = rank + jnp.where(beats, 1, 0)
                caps.append(jnp.where(rank < MOBA_TOPK, POS, NEG))
            cap = jnp.concatenate(caps, axis=1)
        pieces = []
        for j in range(i + 1):
            sj = st[j * MOBA_BLOCK:(j + 1) * MOBA_BLOCK]
            if j == i:
                sj = jnp.where(causal, sj, NEG)
            elif i >= first_ranked:
                sj = jnp.minimum(sj, cap[j:j + 1, :])
            pieces.append(sj)
        st = pieces[0] if i == 0 else jnp.concatenate(pieces, axis=0)
        m_col = jnp.max(st, axis=0, keepdims=True)
        p = jnp.exp2(st - m_col).astype(BF16)
        ot = _dot(vt_aug[:, :(i + 1) * MOBA_BLOCK], p)
        denom = ot[LANES:LANES + 1]
        for h in range(heads):
            rows = slice(h * HEAD_DIM, (h + 1) * HEAD_DIM)
            qcols = slice(h * MOBA_BLOCK, (h + 1) * MOBA_BLOCK)
            ot_ref[rows, i * MOBA_BLOCK:(i + 1) * MOBA_BLOCK] = ot[rows, qcols] / denom[:, qcols]

    st_next = scores(0)
    for i in range(nb):
        st = st_next
        if i + 1 < nb:
            st_next = scores(i + 1)
        finish(i, st)
    o_ref[...] = ot_ref[...].T.astype(BF16)


def _attn(q2, k2, v2, bsz, seq):
    spec = pl.BlockSpec((seq, LANES), lambda b, hp: (b, hp))
    return pl.pallas_call(
        _attn_kernel,
        grid=(bsz, ATTN_WIDTH // LANES),
        in_specs=[spec, spec, spec],
        out_specs=spec,
        out_shape=jax.ShapeDtypeStruct(q2.shape, BF16),
        scratch_shapes=[pltpu.VMEM((LANES, seq), F32)],
        compiler_params=pltpu.CompilerParams(
            dimension_semantics=("arbitrary", "arbitrary"), vmem_limit_bytes=VMEM_LIMIT),
        name="attn",
    )(q2, k2, v2)


def _ordered_after(x, anchor):
    rows, width = anchor.shape
    parts = [anchor[r:r + BF16_ROWS, c:c + LANES]
             for r in range(0, rows, BF16_ROWS) for c in range(0, width, LANES)]
    while len(parts) > 1:
        parts = [jnp.maximum(parts[n], parts[n + 1]) for n in range(0, len(parts), 2)]
    a = parts[0]
    head = x[:BF16_ROWS, :LANES] + (a - a)
    top = jnp.concatenate([head, x[:BF16_ROWS, LANES:]], axis=1)
    return jnp.concatenate([top, x[BF16_ROWS:]], axis=0)


def _tail_kernel(tiles_per_seq, u_ref, halo_ref, x_ref, a_ref, p_ref, cw_ref, cb_ref, lg_ref, lb_ref,
                 wout_ref, gf_ref, wup_ref, wdn_ref, gp_ref, wg_ref, wp_ref, gfin_ref,
                 o_ref, conv_ref):
    i = pl.program_id(0)
    n_tiles = pl.num_programs(0) - 1

    @pl.when(i == 0)
    def _():
        conv_ref[...] = jnp.zeros_like(conv_ref)

    tile = jnp.minimum(i, n_tiles - 1)
    starts_seq = lax.rem(tile, tiles_per_seq) == 0
    halo = jnp.where(starts_seq, 0.0, halo_ref[...])

    def conv_chunk(c, not_before):
        def window(lanes):
            if c == 0:
                return jnp.concatenate([halo[:, lanes], u_ref[0:CONV_ROWS, lanes]], axis=0)
            return u_ref[c * CONV_ROWS - CONV_PAD:(c + 1) * CONV_ROWS, lanes]
        y = _conv_rows(window, cw_ref, cb_ref, lg_ref, lb_ref, not_before)
        conv_ref[c * CONV_ROWS:(c + 1) * CONV_ROWS, :] = y
        return y

    h = (x_ref[...]
         + _dot(conv_ref[...], wout_ref[:CONV_CH, :])
         + _dot(a_ref[...], wout_ref[CONV_CH:, :]))
    hn = _rmsnorm(h, gf_ref[...]).astype(BF16)
    per_group = u_ref.shape[0] // CONV_ROWS // 4
    assert u_ref.shape[0] == 4 * per_group * CONV_ROWS

    def conv_group(q, x, not_before):
        for c in range(q * per_group, (q + 1) * per_group):
            x = _ordered_after(x, conv_chunk(c, not_before))
        return x

    gt = _dot(hn, wup_ref[:, :D_FF])
    hn = conv_group(1, conv_group(0, hn, hn), hn)
    up = _dot(hn, wup_ref[:, D_FF:])
    act = conv_group(2, (jax.nn.silu(gt) * up).astype(BF16), hn)
    h = h + _dot(act, wdn_ref[...])
    hn = conv_group(3, _rmsnorm(h, gp_ref[...]).astype(BF16), act)
    gate = jax.nn.sigmoid(_dot(hn, wg_ref[...]))
    h = h + gate * _dot(p_ref[...].astype(BF16), wp_ref[...])
    o_ref[...] = _rmsnorm(h, gfin_ref[...])


def _tail(u2, x2, attn2, p2, seq, cw, cb, lg, lb, wout, gf, wup, wdn, gp, wg, wp, gfin):
    t = x2.shape[0]
    rows = TAIL_ROWS
    n_tiles = t // rows
    halo_per_tile = rows // CONV_PAD
    cur = lambda i: jnp.minimum(i, n_tiles - 1)
    prev = lambda i: jnp.maximum(i - 1, 0)
    prev_spec = lambda width: pl.BlockSpec((rows, width), lambda i: (prev(i), 0))
    full = lambda a: pl.BlockSpec(a.shape, lambda i: (0, 0), pipeline_mode=pl.Buffered(1))
    return pl.pallas_call(
        functools.partial(_tail_kernel, seq // rows),
        grid=(n_tiles + 1,),
        in_specs=[pl.BlockSpec((rows, CONV_CH), lambda i: (cur(i), 0)),
                  pl.BlockSpec((CONV_PAD, CONV_CH),
                               lambda i: (jnp.maximum(cur(i) * halo_per_tile - 1, 0), 0)),
                  prev_spec(D_MODEL), prev_spec(ATTN_WIDTH), prev_spec(PLE_DIM),
                  full(cw), full(cb), full(lg), full(lb),
                  full(wout), full(gf), full(wup), full(wdn), full(gp), full(wg), full(wp),
                  full(gfin)],
        out_specs=prev_spec(D_MODEL),
        out_shape=jax.ShapeDtypeStruct((t, D_MODEL), F32),
        scratch_shapes=[pltpu.VMEM((rows, CONV_CH), BF16)],
        compiler_params=pltpu.CompilerParams(
            dimension_semantics=("arbitrary",), vmem_limit_bytes=VMEM_LIMIT),
        name="tail",
    )(u2, u2, x2, attn2, p2, cw, cb, lg, lb, wout, gf, wup, wdn, gp, wg, wp, gfin)


def kernel(x, p, positions, norm_mix_g, w_in, conv_w, conv_b, conv_ln_g, conv_ln_b, w_out,
           norm_ffn_g, w_ffn_up, w_ffn_down, norm_ple_g, w_ple_gate, w_ple_proj, final_norm_g):
    bsz, seq, _ = x.shape
    t = bsz * seq
    depth = w_in.shape[0]
    row = lambda a: a.reshape(1, -1)
    inv_freq = ROPE_THETA ** (-jnp.arange(0, ROPE_DIM, 2, dtype=F32) / ROPE_DIM)
    h = x.reshape(t, D_MODEL)
    for i in range(depth):
        u, q, k, v = _in_proj(h, positions, inv_freq, row(norm_mix_g[i]), w_in[i].astype(BF16))
        attn = _attn(q, k, v, bsz, seq)
        assert depth == 1
        h = _tail(u, h, attn, p[i].reshape(t, PLE_DIM), seq, conv_w[i], row(conv_b[i]),
                  row(conv_ln_g[i]), row(conv_ln_b[i]), w_out[i].astype(BF16),
                  row(norm_ffn_g[i]), w_ffn_up[i].astype(BF16), w_ffn_down[i].astype(BF16),
                  row(norm_ple_g[i]), w_ple_gate[i].astype(BF16), w_ple_proj[i].astype(BF16),
                  row(final_norm_g))
    return h.reshape(bsz, seq, D_MODEL)
```

```python
import functools

import jax
import jax.numpy as jnp
from jax import lax
from jax.experimental import pallas as pl
from jax.experimental.pallas import tpu as pltpu

D_MODEL = 1024
CONV_CH = 512
ATTN_HEADS = 8
HEAD_DIM = 64
ATTN_WIDTH = ATTN_HEADS * HEAD_DIM
IN_WIDTH = 2 * CONV_CH + 3 * ATTN_WIDTH
CONV_KERNEL = 31
MOBA_BLOCK = 256
MOBA_TOPK = 3
ROPE_THETA = 500000.0
ROPE_DIM = HEAD_DIM // 4
ROPE_HALF = ROPE_DIM // 2
D_FF = -(-8 * D_MODEL // (3 * 256)) * 256
PLE_DIM = 256
EPS = 1e-6

LANES = 128
SUBLANES = 8
BF16_ROWS = 16
LOG2E = 1.4426950408889634
HEADS_PER_STEP = LANES // HEAD_DIM
CONV_PAD = 32
CONV_ROWS = 64
IN_ROWS = 512
TAIL_ROWS = 256
VMEM_LIMIT = 56 * 1024 * 1024

F32 = jnp.float32
BF16 = jnp.bfloat16
NEG = float(jnp.finfo(jnp.float32).min)
POS = float(jnp.finfo(jnp.float32).max)


def _rmsnorm(x, g):
    return x * lax.rsqrt(jnp.mean(x * x, axis=-1, keepdims=True) + EPS) * g


def _dot(a, b):
    return jnp.dot(a, b, preferred_element_type=F32)


def _dot_nt(a, b):
    return lax.dot_general(a, b, (((1,), (1,)), ((), ())), preferred_element_type=F32)


def _rope_expand_matrix():
    r = jnp.arange(LANES)[:, None]
    l = jnp.arange(2 * LANES)[None, :]
    k = r % (2 * ROPE_HALF)
    piece_ok = r < 3 * 2 * ROPE_HALF
    d = (l % LANES) % HEAD_DIM
    want = d % ROPE_HALF + ROPE_HALF * (l // LANES)
    return (piece_ok & (d < ROPE_DIM) & (k == want)).astype(BF16)


def _in_proj_kernel(x_ref, pos_ref, freq_ref, expand_ref, g_ref, w32_ref,
                    u_ref, q_ref, k_ref, v_ref, w_ref):
    @pl.when(pl.program_id(0) == 0)
    def _():
        w_ref[...] = w32_ref[...].astype(BF16)

    rows = x_ref.shape[0]
    half = rows // 2
    zs = []
    for r in range(2):
        hn = _rmsnorm(x_ref[r * half:(r + 1) * half, :], g_ref[...]).astype(BF16)
        zs.append(_dot(hn, w_ref[...]))

    ang = pos_ref[0].astype(F32) * freq_ref[...]
    small = jnp.concatenate([jnp.cos(ang), jnp.sin(ang)], axis=0)
    hi = small.astype(BF16).astype(F32)
    mid = (small - hi).astype(BF16).astype(F32)
    lo = (small - hi - mid).astype(BF16).astype(F32)
    pad = jnp.zeros((LANES - 3 * small.shape[0], small.shape[1]), F32)
    pieces = jnp.concatenate([hi, mid, lo, pad], axis=0).T.astype(BF16)
    table = _dot(pieces, expand_ref[...])
    d = lax.broadcasted_iota(jnp.int32, (1, LANES), 1) & (HEAD_DIM - 1)
    cos = table[:, :LANES] + jnp.where(d < ROPE_DIM, 0.0, 1.0)
    sin = table[:, LANES:]
    sin_from_hi = jnp.where(d < ROPE_HALF, -sin, 0.0)
    sin_from_lo = jnp.where(d >= ROPE_HALF, sin, 0.0)

    q0 = 2 * CONV_CH
    k0 = q0 + ATTN_WIDTH
    v0 = k0 + ATTN_WIDTH
    scale = HEAD_DIM ** -0.5 * LOG2E
    for r, z in enumerate(zs):
        rs = slice(r * half, (r + 1) * half)
        u_ref[rs, :] = z[:, :CONV_CH] * jax.nn.sigmoid(z[:, CONV_CH:2 * CONV_CH])

        def rope(t, rs=rs):
            return (t * cos[rs]
                    + pltpu.roll(t, ROPE_HALF, 1) * sin_from_lo[rs]
                    + pltpu.roll(t, LANES - ROPE_HALF, 1) * sin_from_hi[rs])

        for c in range(ATTN_WIDTH // LANES):
            sl = slice(c * LANES, (c + 1) * LANES)
            q_ref[rs, sl] = (rope(z[:, q0 + c * LANES:q0 + (c + 1) * LANES]) * scale).astype(BF16)
            k_ref[rs, sl] = rope(z[:, k0 + c * LANES:k0 + (c + 1) * LANES]).astype(BF16)
        v_ref[rs, :] = z[:, v0:v0 + ATTN_WIDTH].astype(BF16)


def _in_proj(x2, positions, inv_freq, g, w):
    t = x2.shape[0]
    rows = IN_ROWS
    pos3 = positions.reshape(t // rows, 1, rows)
    freq = jnp.broadcast_to(inv_freq[:, None], (ROPE_HALF, rows))
    expand = _rope_expand_matrix()
    row_spec = lambda width: pl.BlockSpec((rows, width), lambda i: (i, 0))
    full = lambda a: pl.BlockSpec(a.shape, lambda i: (0, 0))
    return pl.pallas_call(
        _in_proj_kernel,
        grid=(t // rows,),
        in_specs=[row_spec(D_MODEL), pl.BlockSpec((1, 1, rows), lambda i: (i, 0, 0)),
                  full(freq), full(expand), full(g),
                  pl.BlockSpec(w.shape, lambda i: (0, 0), pipeline_mode=pl.Buffered(1))],
        out_specs=[row_spec(CONV_CH), row_spec(ATTN_WIDTH), row_spec(ATTN_WIDTH), row_spec(ATTN_WIDTH)],
        out_shape=[jax.ShapeDtypeStruct((t, CONV_CH), F32),
                   jax.ShapeDtypeStruct((t, ATTN_WIDTH), BF16),
                   jax.ShapeDtypeStruct((t, ATTN_WIDTH), BF16),
                   jax.ShapeDtypeStruct((t, ATTN_WIDTH), BF16)],
        scratch_shapes=[pltpu.VMEM(w.shape, BF16)],
        compiler_params=pltpu.CompilerParams(
            dimension_semantics=("arbitrary",), vmem_limit_bytes=VMEM_LIMIT),
        name="in_proj",
    )(x2, pos3, freq, expand, g, w)


def _conv_rows(window, w_ref, b_ref, lg_ref, lb_ref, not_before):
    first = CONV_PAD - (CONV_KERNEL - 1)
    win_rows = CONV_ROWS + CONV_PAD
    a = not_before[:BF16_ROWS, :LANES].astype(F32)[0:1, :]
    hold = a - a
    accs = []
    for g in range(CONV_CH // LANES):
        lanes = slice(g * LANES, (g + 1) * LANES)
        win = window(lanes)
        acc = jnp.broadcast_to(b_ref[:, lanes] + hold, (CONV_ROWS, LANES))
        for phase in range(SUBLANES):
            shifted = win if phase == 0 else pltpu.roll(win, win_rows - phase, 0)
            for base in range(0, CONV_PAD + 1, SUBLANES):
                j = base + phase - first
                if 0 <= j < CONV_KERNEL:
                    acc = acc + w_ref[j:j + 1, lanes] * shifted[base:base + CONV_ROWS]
        accs.append(acc)
        hold = acc[0:1, :] - acc[0:1, :]
    acc = jnp.concatenate(accs, axis=1)
    mu = jnp.mean(acc, axis=-1, keepdims=True)
    dev = acc - mu
    var = jnp.mean(dev * dev, axis=-1, keepdims=True)
    y = dev * lax.rsqrt(var + EPS) * lg_ref[...] + lb_ref[...]
    return (y * jax.nn.sigmoid(y)).astype(BF16)


def _attn_kernel(q_ref, k_ref, v_ref, o_ref, ot_ref):
    seq = q_ref.shape[0]
    nb = seq // MOBA_BLOCK
    q = q_ref[...]
    k = k_ref[...]
    vt = v_ref[...].astype(F32).T.astype(BF16)
    lane = lax.broadcasted_iota(jnp.int32, (1, LANES), 1)
    kmean = jnp.mean(k.astype(F32).reshape(nb, MOBA_BLOCK, LANES), axis=1)
    heads = HEADS_PER_STEP
    cols = heads * MOBA_BLOCK
    key_row = lax.broadcasted_iota(jnp.int32, (MOBA_BLOCK, cols), 0)
    q_col = lax.broadcasted_iota(jnp.int32, (MOBA_BLOCK, cols), 1) & (MOBA_BLOCK - 1)
    causal = key_row <= q_col
    blk = lax.broadcasted_iota(jnp.int32, (nb, MOBA_BLOCK), 0)
    first_ranked = MOBA_TOPK + 1
    vt_aug = jnp.concatenate([vt, jnp.ones((BF16_ROWS, seq), BF16)], axis=0)

    in_head = [(lane >= h * HEAD_DIM) & (lane < (h + 1) * HEAD_DIM) for h in range(heads)]
    gate_t = [_dot_nt(jnp.where(m, kmean, 0.0).astype(BF16), q[first_ranked * MOBA_BLOCK:])
              for m in in_head]

    def scores(i):
        qi = q[i * MOBA_BLOCK:(i + 1) * MOBA_BLOCK]
        q2 = jnp.concatenate([jnp.where(m, qi, jnp.zeros_like(qi)) for m in in_head], axis=0)
        return _dot_nt(k[:(i + 1) * MOBA_BLOCK], q2)

    def finish(i, st):
        if i >= first_ranked:
            caps = []
            for h in range(heads):
                c0 = (i - first_ranked) * MOBA_BLOCK
                gi = gate_t[h][:, c0:c0 + MOBA_BLOCK]
                rank = jnp.zeros((nb, MOBA_BLOCK), jnp.int32)
                for m in range(i):
                    gm = gi[m:m + 1, :]
                    beats = (gm > gi) | ((gm == gi) & (blk > m))
                    rank = rank + jnp.where(beats, 1, 0)
                caps.append(jnp.where(rank < MOBA_TOPK, POS, NEG))
            cap = jnp.concatenate(caps, axis=1)
        pieces = []
        for j in range(i + 1):
            sj = st[j * MOBA_BLOCK:(j + 1) * MOBA_BLOCK]
            if j == i:
                sj = jnp.where(causal, sj, NEG)
            elif i >= first_ranked:
                sj = jnp.minimum(sj, cap[j:j + 1, :])
            pieces.append(sj)
        st = pieces[0] if i == 0 else jnp.concatenate(pieces, axis=0)
        m_col = jnp.max(st, axis=0, keepdims=True)
        p = jnp.exp2(st - m_col).astype(BF16)
        ot = _dot(vt_aug[:, :(i + 1) * MOBA_BLOCK], p)
        denom = ot[LANES:LANES + 1]
        for h in range(heads):
            rows = slice(h * HEAD_DIM, (h + 1) * HEAD_DIM)
            qcols = slice(h * MOBA_BLOCK, (h + 1) * MOBA_BLOCK)
            ot_ref[rows, i * MOBA_BLOCK:(i + 1) * MOBA_BLOCK] = ot[rows, qcols] / denom[:, qcols]

    st_next = scores(0)
    for i in range(nb):
        st = st_next
        if i + 1 < nb:
            st_next = scores(i + 1)
        finish(i, st)
    o_ref[...] = ot_ref[...].T.astype(BF16)


def _attn(q2, k2, v2, bsz, seq):
    spec = pl.BlockSpec((seq, LANES), lambda b, hp: (b, hp))
    return pl.pallas_call(
        _attn_kernel,
        grid=(bsz, ATTN_WIDTH // LANES),
        in_specs=[spec, spec, spec],
        out_specs=spec,
        out_shape=jax.ShapeDtypeStruct(q2.shape, BF16),
        scratch_shapes=[pltpu.VMEM((LANES, seq), F32)],
        compiler_params=pltpu.CompilerParams(
            dimension_semantics=("arbitrary", "arbitrary"), vmem_limit_bytes=VMEM_LIMIT),
        name="attn",
    )(q2, k2, v2)


def _ordered_after(x, anchor):
    rows, width = anchor.shape
    parts = [anchor[r:r + BF16_ROWS, c:c + LANES]
             for r in range(0, rows, BF16_ROWS) for c in range(0, width, LANES)]
    while len(parts) > 1:
        parts = [jnp.maximum(parts[n], parts[n + 1]) for n in range(0, len(parts), 2)]
    a = parts[0]
    head = x[:BF16_ROWS, :LANES] + (a - a)
    top = jnp.concatenate([head, x[:BF16_ROWS, LANES:]], axis=1)
    return jnp.concatenate([top, x[BF16_ROWS:]], axis=0)


def _tail_kernel(tiles_per_seq, u_ref, halo_ref, x_ref, a_ref, p_ref, cw_ref, cb_ref, lg_ref, lb_ref,
                 wout32_ref, gf_ref, wup_ref, wdn32_ref, gp_ref, wg32_ref, wp32_ref, gfin_ref,
                 o_ref, conv_ref, wout_ref, wdn_ref, wg_ref, wp_ref):
    i = pl.program_id(0)
    n_tiles = pl.num_programs(0) - 1

    @pl.when(i == 0)
    def _():
        conv_ref[...] = jnp.zeros_like(conv_ref)
        for src, dst in ((wout32_ref, wout_ref), (wdn32_ref, wdn_ref),
                         (wg32_ref, wg_ref), (wp32_ref, wp_ref)):
            dst[...] = src[...].astype(BF16)

    tile = jnp.minimum(i, n_tiles - 1)
    starts_seq = lax.rem(tile, tiles_per_seq) == 0
    halo = jnp.where(starts_seq, 0.0, halo_ref[...])

    def conv_chunk(c, not_before):
        def window(lanes):
            if c == 0:
                return jnp.concatenate([halo[:, lanes], u_ref[0:CONV_ROWS, lanes]], axis=0)
            return u_ref[c * CONV_ROWS - CONV_PAD:(c + 1) * CONV_ROWS, lanes]
        y = _conv_rows(window, cw_ref, cb_ref, lg_ref, lb_ref, not_before)
        conv_ref[c * CONV_ROWS:(c + 1) * CONV_ROWS, :] = y
        return y

    h = (x_ref[...]
         + _dot(conv_ref[...], wout_ref[:CONV_CH, :])
         + _dot(a_ref[...], wout_ref[CONV_CH:, :]))
    hn = _rmsnorm(h, gf_ref[...]).astype(BF16)
    per_group = u_ref.shape[0] // CONV_ROWS // 4
    assert u_ref.shape[0] == 4 * per_group * CONV_ROWS

    def conv_group(q, x, not_before):
        for c in range(q * per_group, (q + 1) * per_group):
            y = conv_chunk(c, not_before)
            x = _ordered_after(x, y)
            not_before = y
        return x

    gt = _dot(hn, wup_ref[:, :D_FF])
    hn0 = conv_group(0, hn, hn)
    hn = conv_group(1, hn0, hn0)
    up = _dot(hn, wup_ref[:, D_FF:])
    act = conv_group(2, (jax.nn.silu(gt) * up).astype(BF16), hn)
    h = h + _dot(act, wdn_ref[...])
    hn = conv_group(3, _rmsnorm(h, gp_ref[...]).astype(BF16), act)
    gate = jax.nn.sigmoid(_dot(hn, wg_ref[...]))
    h = h + gate * _dot(p_ref[...].astype(BF16), wp_ref[...])
    o_ref[...] = _rmsnorm(h, gfin_ref[...])


def _tail(u2, x2, attn2, p2, seq, cw, cb, lg, lb, wout, gf, wup, wdn, gp, wg, wp, gfin):
    t = x2.shape[0]
    rows = TAIL_ROWS
    n_tiles = t // rows
    halo_per_tile = rows // CONV_PAD
    cur = lambda i: jnp.minimum(i, n_tiles - 1)
    prev = lambda i: jnp.maximum(i - 1, 0)
    prev_spec = lambda width: pl.BlockSpec((rows, width), lambda i: (prev(i), 0))
    full = lambda a: pl.BlockSpec(a.shape, lambda i: (0, 0), pipeline_mode=pl.Buffered(1))
    return pl.pallas_call(
        functools.partial(_tail_kernel, seq // rows),
        grid=(n_tiles + 1,),
        in_specs=[pl.BlockSpec((rows, CONV_CH), lambda i: (cur(i), 0)),
                  pl.BlockSpec((CONV_PAD, CONV_CH),
                               lambda i: (jnp.maximum(cur(i) * halo_per_tile - 1, 0), 0)),
                  prev_spec(D_MODEL), prev_spec(ATTN_WIDTH), prev_spec(PLE_DIM),
                  full(cw), full(cb), full(lg), full(lb),
                  full(wout), full(gf), full(wup), full(wdn), full(gp), full(wg), full(wp),
                  full(gfin)],
        out_specs=prev_spec(D_MODEL),
        out_shape=jax.ShapeDtypeStruct((t, D_MODEL), F32),
        scratch_shapes=[pltpu.VMEM((rows, CONV_CH), BF16)]
        + [pltpu.VMEM(w32.shape, BF16) for w32 in (wout, wdn, wg, wp)],
        compiler_params=pltpu.CompilerParams(
            dimension_semantics=("arbitrary",), vmem_limit_bytes=VMEM_LIMIT),
        name="tail",
    )(u2, u2, x2, attn2, p2, cw, cb, lg, lb, wout, gf, wup, wdn, gp, wg, wp, gfin)


def kernel(x, p, positions, norm_mix_g, w_in, conv_w, conv_b, conv_ln_g, conv_ln_b, w_out,
           norm_ffn_g, w_ffn_up, w_ffn_down, norm_ple_g, w_ple_gate, w_ple_proj, final_norm_g):
    bsz, seq, _ = x.shape
    t = bsz * seq
    depth = w_in.shape[0]
    row = lambda a: a.reshape(1, -1)
    inv_freq = ROPE_THETA ** (-jnp.arange(0, ROPE_DIM, 2, dtype=F32) / ROPE_DIM)
    h = x.reshape(t, D_MODEL)
    for i in range(depth):
        u, q, k, v = _in_proj(h, positions, inv_freq, row(norm_mix_g[i]), w_in[i])
        attn = _attn(q, k, v, bsz, seq)
        assert depth == 1
        h = _tail(u, h, attn, p[i].reshape(t, PLE_DIM), seq, conv_w[i], row(conv_b[i]),
                  row(conv_ln_g[i]), row(conv_ln_b[i]), w_out[i],
                  row(norm_ffn_g[i]), w_ffn_up[i].astype(BF16), w_ffn_down[i],
                  row(norm_ple_g[i]), w_ple_gate[i], w_ple_proj[i],
                  row(final_norm_g))
    return h.reshape(bsz, seq, D_MODEL)
```

```python
import functools

import jax
import jax.numpy as jnp
from jax import lax
from jax.experimental import pallas as pl
from jax.experimental.pallas import tpu as pltpu

D_MODEL = 1024
CONV_CH = 512
ATTN_HEADS = 8
HEAD_DIM = 64
ATTN_WIDTH = ATTN_HEADS * HEAD_DIM
IN_WIDTH = 2 * CONV_CH + 3 * ATTN_WIDTH
CONV_KERNEL = 31
MOBA_BLOCK = 256
MOBA_TOPK = 3
ROPE_THETA = 500000.0
ROPE_DIM = HEAD_DIM // 4
ROPE_HALF = ROPE_DIM // 2
D_FF = -(-8 * D_MODEL // (3 * 256)) * 256
PLE_DIM = 256
EPS = 1e-6

LANES = 128
SUBLANES = 8
BF16_ROWS = 16
LOG2E = 1.4426950408889634
HEADS_PER_STEP = LANES // HEAD_DIM
CONV_PAD = 32
CONV_ROWS = 64
IN_ROWS = 512
TAIL_ROWS = 256
VMEM_LIMIT = 56 * 1024 * 1024

F32 = jnp.float32
BF16 = jnp.bfloat16
NEG = float(jnp.finfo(jnp.float32).min)
POS = float(jnp.finfo(jnp.float32).max)


def _rmsnorm(x, g):
    return x * lax.rsqrt(jnp.mean(x * x, axis=-1, keepdims=True) + EPS) * g


def _dot(a, b):
    return jnp.dot(a, b, preferred_element_type=F32)


def _dot_nt(a, b):
    return lax.dot_general(a, b, (((1,), (1,)), ((), ())), preferred_element_type=F32)


def _rope_expand_matrix():
    r = jnp.arange(LANES)[:, None]
    l = jnp.arange(2 * LANES)[None, :]
    k = r % (2 * ROPE_HALF)
    piece_ok = r < 3 * 2 * ROPE_HALF
    d = (l % LANES) % HEAD_DIM
    want = d % ROPE_HALF + ROPE_HALF * (l // LANES)
    return (piece_ok & (d < ROPE_DIM) & (k == want)).astype(BF16)


def _in_proj_kernel(x_ref, pos_ref, freq_ref, expand_ref, g_ref, w32_ref,
                    u_ref, q_ref, k_ref, v_ref, w_ref):
    @pl.when(pl.program_id(0) == 0)
    def _():
        w_ref[...] = w32_ref[...].astype(BF16)

    rows = x_ref.shape[0]
    half = rows // 2
    zs = []
    for r in range(2):
        hn = _rmsnorm(x_ref[r * half:(r + 1) * half, :], g_ref[...]).astype(BF16)
        zs.append(_dot(hn, w_ref[...]))

    ang = pos_ref[0].astype(F32) * freq_ref[...]
    small = jnp.concatenate([jnp.cos(ang), jnp.sin(ang)], axis=0)
    hi = small.astype(BF16).astype(F32)
    mid = (small - hi).astype(BF16).astype(F32)
    lo = (small - hi - mid).astype(BF16).astype(F32)
    pad = jnp.zeros((LANES - 3 * small.shape[0], small.shape[1]), F32)
    pieces = jnp.concatenate([hi, mid, lo, pad], axis=0).T.astype(BF16)
    table = _dot(pieces, expand_ref[...])
    d = lax.broadcasted_iota(jnp.int32, (1, LANES), 1) & (HEAD_DIM - 1)
    cos = table[:, :LANES] + jnp.where(d < ROPE_DIM, 0.0, 1.0)
    sin = table[:, LANES:]
    sin_from_hi = jnp.where(d < ROPE_HALF, -sin, 0.0)
    sin_from_lo = jnp.where(d >= ROPE_HALF, sin, 0.0)

    q0 = 2 * CONV_CH
    k0 = q0 + ATTN_WIDTH
    v0 = k0 + ATTN_WIDTH
    scale = HEAD_DIM ** -0.5 * LOG2E
    for r, z in enumerate(zs):
        rs = slice(r * half, (r + 1) * half)
        u_ref[rs, :] = z[:, :CONV_CH] * jax.nn.sigmoid(z[:, CONV_CH:2 * CONV_CH])

        def rope(t, rs=rs):
            return (t * cos[rs]
                    + pltpu.roll(t, ROPE_HALF, 1) * sin_from_lo[rs]
                    + pltpu.roll(t, LANES - ROPE_HALF, 1) * sin_from_hi[rs])

        for c in range(ATTN_WIDTH // LANES):
            sl = slice(c * LANES, (c + 1) * LANES)
            q_ref[rs, sl] = (rope(z[:, q0 + c * LANES:q0 + (c + 1) * LANES]) * scale).astype(BF16)
            k_ref[rs, sl] = rope(z[:, k0 + c * LANES:k0 + (c + 1) * LANES]).astype(BF16)
        v_ref[rs, :] = z[:, v0:v0 + ATTN_WIDTH].astype(BF16)


def _in_proj(x2, positions, inv_freq, g, w):
    t = x2.shape[0]
    rows = IN_ROWS
    pos3 = positions.reshape(t // rows, 1, rows)
    freq = jnp.broadcast_to(inv_freq[:, None], (ROPE_HALF, rows))
    expand = _rope_expand_matrix()
    row_spec = lambda width: pl.BlockSpec((rows, width), lambda i: (i, 0))
    full = lambda a: pl.BlockSpec(a.shape, lambda i: (0, 0))
    return pl.pallas_call(
        _in_proj_kernel,
        grid=(t // rows,),
        in_specs=[row_spec(D_MODEL), pl.BlockSpec((1, 1, rows), lambda i: (i, 0, 0)),
                  full(freq), full(expand), full(g),
                  pl.BlockSpec(w.shape, lambda i: (0, 0), pipeline_mode=pl.Buffered(1))],
        out_specs=[row_spec(CONV_CH), row_spec(ATTN_WIDTH), row_spec(ATTN_WIDTH), row_spec(ATTN_WIDTH)],
        out_shape=[jax.ShapeDtypeStruct((t, CONV_CH), F32),
                   jax.ShapeDtypeStruct((t, ATTN_WIDTH), BF16),
                   jax.ShapeDtypeStruct((t, ATTN_WIDTH), BF16),
                   jax.ShapeDtypeStruct((t, ATTN_WIDTH), BF16)],
        scratch_shapes=[pltpu.VMEM(w.shape, BF16)],
        compiler_params=pltpu.CompilerParams(
            dimension_semantics=("arbitrary",), vmem_limit_bytes=VMEM_LIMIT),
        name="in_proj",
    )(x2, pos3, freq, expand, g, w)


def _conv_rows(window, w_ref, b_ref, lg_ref, lb_ref, not_before):
    first = CONV_PAD - (CONV_KERNEL - 1)
    win_rows = CONV_ROWS + CONV_PAD
    a = not_before[:BF16_ROWS, :LANES].astype(F32)[0:1, :]
    hold = a - a
    accs = []
    for g in range(CONV_CH // LANES):
        lanes = slice(g * LANES, (g + 1) * LANES)
        win = window(lanes)
        acc = jnp.broadcast_to(b_ref[:, lanes] + hold, (CONV_ROWS, LANES))
        for phase in range(SUBLANES):
            shifted = win if phase == 0 else pltpu.roll(win, win_rows - phase, 0)
            for base in range(0, CONV_PAD + 1, SUBLANES):
                j = base + phase - first
                if 0 <= j < CONV_KERNEL:
                    acc = acc + w_ref[j:j + 1, lanes] * shifted[base:base + CONV_ROWS]
        accs.append(acc)
        hold = acc[0:1, :] - acc[0:1, :]
    acc = jnp.concatenate(accs, axis=1)
    mu = jnp.mean(acc, axis=-1, keepdims=True)
    dev = acc - mu
    var = jnp.mean(dev * dev, axis=-1, keepdims=True)
    y = dev * lax.rsqrt(var + EPS) * lg_ref[...] + lb_ref[...]
    return (y * jax.nn.sigmoid(y)).astype(BF16)


def _attn_kernel(n_weights, q_ref, k_ref, v_ref, *refs):
    w32_refs = refs[:n_weights]
    o_ref = refs[n_weights]
    w16_refs = refs[n_weights + 1:2 * n_weights + 1]
    ot_ref = refs[2 * n_weights + 1]
    for src, dst in zip(w32_refs, w16_refs):
        dst[...] = src[...].astype(BF16)

    seq = q_ref.shape[0]
    nb = seq // MOBA_BLOCK
    q = q_ref[...]
    k = k_ref[...]
    vt = v_ref[...].astype(F32).T.astype(BF16)
    lane = lax.broadcasted_iota(jnp.int32, (1, LANES), 1)
    kmean = jnp.mean(k.astype(F32).reshape(nb, MOBA_BLOCK, LANES), axis=1)
    heads = HEADS_PER_STEP
    cols = heads * MOBA_BLOCK
    key_row = lax.broadcasted_iota(jnp.int32, (MOBA_BLOCK, cols), 0)
    q_col = lax.broadcasted_iota(jnp.int32, (MOBA_BLOCK, cols), 1) & (MOBA_BLOCK - 1)
    causal = key_row <= q_col
    blk = lax.broadcasted_iota(jnp.int32, (nb, MOBA_BLOCK), 0)
    first_ranked = MOBA_TOPK + 1
    vt_aug = jnp.concatenate([vt, jnp.ones((BF16_ROWS, seq), BF16)], axis=0)

    in_head = [(lane >= h * HEAD_DIM) & (lane < (h + 1) * HEAD_DIM) for h in range(heads)]
    gate_t = [_dot_nt(jnp.where(m, kmean, 0.0).astype(BF16), q[first_ranked * MOBA_BLOCK:])
              for m in in_head]

    def scores(i):
        qi = q[i * MOBA_BLOCK:(i + 1) * MOBA_BLOCK]
        q2 = jnp.concatenate([jnp.where(m, qi, jnp.zeros_like(qi)) for m in in_head], axis=0)
        return _dot_nt(k[:(i + 1) * MOBA_BLOCK], q2)

    def finish(i, st):
        if i >= first_ranked:
            caps = []
            for h in range(heads):
                c0 = (i - first_ranked) * MOBA_BLOCK
                gi = gate_t[h][:, c0:c0 + MOBA_BLOCK]
                rank = jnp.zeros((nb, MOBA_BLOCK), jnp.int32)
                for m in range(i):
                    gm = gi[m:m + 1, :]
                    beats = (gm > gi) | ((gm == gi) & (blk > m))
                    rank = rank + jnp.where(beats, 1, 0)
                caps.append(jnp.where(rank < MOBA_TOPK, POS, NEG))
            cap = jnp.concatenate(caps, axis=1)
        pieces = []
        for j in range(i + 1):
            sj = st[j * MOBA_BLOCK:(j + 1) * MOBA_BLOCK]
            if j == i:
                sj = jnp.where(causal, sj, NEG)
            elif i >= first_ranked:
                sj = jnp.minimum(sj, cap[j:j + 1, :])
            pieces.append(sj)
        st = pieces[0] if i == 0 else jnp.concatenate(pieces, axis=0)
        m_col = jnp.max(st, axis=0, keepdims=True)
        p = jnp.exp2(st - m_col).astype(BF16)
        ot = _dot(vt_aug[:, :(i + 1) * MOBA_BLOCK], p)
        denom = ot[LANES:LANES + 1]
        for h in range(heads):
            rows = slice(h * HEAD_DIM, (h + 1) * HEAD_DIM)
            qcols = slice(h * MOBA_BLOCK, (h + 1) * MOBA_BLOCK)
            ot_ref[rows, i * MOBA_BLOCK:(i + 1) * MOBA_BLOCK] = ot[rows, qcols] / denom[:, qcols]

    st_next = scores(0)
    for i in range(nb):
        st = st_next
        if i + 1 < nb:
            st_next = scores(i + 1)
        finish(i, st)
    o_ref[...] = ot_ref[...].T.astype(BF16)


def _attn(q2, k2, v2, bsz, seq, weights):
    pairs = ATTN_WIDTH // LANES
    n_steps = bsz * pairs
    spec = pl.BlockSpec((seq, LANES), lambda b, hp: (b, hp))

    def slice_spec(w):
        n = next(n for n in (n_steps, n_steps // 2, n_steps // 4, 1)
                 if w.shape[0] % n == 0 and (w.shape[0] // n) % BF16_ROWS == 0)
        return pl.BlockSpec((w.shape[0] // n, w.shape[1]),
                            lambda b, hp: (jnp.minimum(b * pairs + hp, n - 1), 0))

    w_specs = [slice_spec(w) for w in weights]
    out = pl.pallas_call(
        functools.partial(_attn_kernel, len(weights)),
        grid=(bsz, pairs),
        in_specs=[spec, spec, spec] + w_specs,
        out_specs=[spec] + w_specs,
        out_shape=[jax.ShapeDtypeStruct(q2.shape, BF16)]
        + [jax.ShapeDtypeStruct(w.shape, BF16) for w in weights],
        scratch_shapes=[pltpu.VMEM((LANES, seq), F32)],
        compiler_params=pltpu.CompilerParams(
            dimension_semantics=("arbitrary", "arbitrary"), vmem_limit_bytes=VMEM_LIMIT),
        name="attn",
    )(q2, k2, v2, *weights)
    return out[0], out[1:]


def _ordered_after(x, anchor):
    rows, width = anchor.shape
    parts = [anchor[r:r + BF16_ROWS, c:c + LANES]
             for r in range(0, rows, BF16_ROWS) for c in range(0, width, LANES)]
    while len(parts) > 1:
        parts = [jnp.maximum(parts[n], parts[n + 1]) for n in range(0, len(parts), 2)]
    a = parts[0]
    head = x[:BF16_ROWS, :LANES] + (a - a)
    top = jnp.concatenate([head, x[:BF16_ROWS, LANES:]], axis=1)
    return jnp.concatenate([top, x[BF16_ROWS:]], axis=0)


def _tail_kernel(tiles_per_seq, u_ref, halo_ref, x_ref, a_ref, p_ref, cw_ref, cb_ref, lg_ref, lb_ref,
                 wout_ref, gf_ref, wup_ref, wdn_ref, gp_ref, wg_ref, wp_ref, gfin_ref,
                 o_ref, conv_ref):
    i = pl.program_id(0)
    n_tiles = pl.num_programs(0) - 1

    @pl.when(i == 0)
    def _():
        conv_ref[...] = jnp.zeros_like(conv_ref)

    tile = jnp.minimum(i, n_tiles - 1)
    starts_seq = lax.rem(tile, tiles_per_seq) == 0
    halo = jnp.where(starts_seq, 0.0, halo_ref[...])

    def conv_chunk(c, not_before):
        def window(lanes):
            if c == 0:
                return jnp.concatenate([halo[:, lanes], u_ref[0:CONV_ROWS, lanes]], axis=0)
            return u_ref[c * CONV_ROWS - CONV_PAD:(c + 1) * CONV_ROWS, lanes]
        y = _conv_rows(window, cw_ref, cb_ref, lg_ref, lb_ref, not_before)
        conv_ref[c * CONV_ROWS:(c + 1) * CONV_ROWS, :] = y
        return y

    h = (x_ref[...]
         + _dot(conv_ref[...], wout_ref[:CONV_CH, :])
         + _dot(a_ref[...], wout_ref[CONV_CH:, :]))
    hn = _rmsnorm(h, gf_ref[...]).astype(BF16)
    per_group = u_ref.shape[0] // CONV_ROWS // 4
    assert u_ref.shape[0] == 4 * per_group * CONV_ROWS

    def conv_group(q, x, not_before):
        for c in range(q * per_group, (q + 1) * per_group):
            y = conv_chunk(c, not_before)
            x = _ordered_after(x, y)
            not_before = y
        return x

    gt = _dot(hn, wup_ref[:, :D_FF])
    hn0 = conv_group(0, hn, hn)
    hn = conv_group(1, hn0, hn0)
    up = _dot(hn, wup_ref[:, D_FF:])
    act = conv_group(2, (jax.nn.silu(gt) * up).astype(BF16), hn)
    h = h + _dot(act, wdn_ref[...])
    hn = conv_group(3, _rmsnorm(h, gp_ref[...]).astype(BF16), act)
    gate = jax.nn.sigmoid(_dot(hn, wg_ref[...]))
    h = h + gate * _dot(p_ref[...].astype(BF16), wp_ref[...])
    o_ref[...] = _rmsnorm(h, gfin_ref[...])


def _tail(u2, x2, attn2, p2, seq, cw, cb, lg, lb, wout, gf, wup, wdn, gp, wg, wp, gfin):
    t = x2.shape[0]
    rows = TAIL_ROWS
    n_tiles = t // rows
    halo_per_tile = rows // CONV_PAD
    cur = lambda i: jnp.minimum(i, n_tiles - 1)
    prev = lambda i: jnp.maximum(i - 1, 0)
    prev_spec = lambda width: pl.BlockSpec((rows, width), lambda i: (prev(i), 0))
    full = lambda a: pl.BlockSpec(a.shape, lambda i: (0, 0), pipeline_mode=pl.Buffered(1))
    return pl.pallas_call(
        functools.partial(_tail_kernel, seq // rows),
        grid=(n_tiles + 1,),
        in_specs=[pl.BlockSpec((rows, CONV_CH), lambda i: (cur(i), 0)),
                  pl.BlockSpec((CONV_PAD, CONV_CH),
                               lambda i: (jnp.maximum(cur(i) * halo_per_tile - 1, 0), 0)),
                  prev_spec(D_MODEL), prev_spec(ATTN_WIDTH), prev_spec(PLE_DIM),
                  full(cw), full(cb), full(lg), full(lb),
                  full(wout), full(gf), full(wup), full(wdn), full(gp), full(wg), full(wp),
                  full(gfin)],
        out_specs=prev_spec(D_MODEL),
        out_shape=jax.ShapeDtypeStruct((t, D_MODEL), F32),
        scratch_shapes=[pltpu.VMEM((rows, CONV_CH), BF16)],
        compiler_params=pltpu.CompilerParams(
            dimension_semantics=("arbitrary",), vmem_limit_bytes=VMEM_LIMIT),
        name="tail",
    )(u2, u2, x2, attn2, p2, cw, cb, lg, lb, wout, gf, wup, wdn, gp, wg, wp, gfin)


def kernel(x, p, positions, norm_mix_g, w_in, conv_w, conv_b, conv_ln_g, conv_ln_b, w_out,
           norm_ffn_g, w_ffn_up, w_ffn_down, norm_ple_g, w_ple_gate, w_ple_proj, final_norm_g):
    bsz, seq, _ = x.shape
    t = bsz * seq
    depth = w_in.shape[0]
    row = lambda a: a.reshape(1, -1)
    inv_freq = ROPE_THETA ** (-jnp.arange(0, ROPE_DIM, 2, dtype=F32) / ROPE_DIM)
    h = x.reshape(t, D_MODEL)
    for i in range(depth):
        u, q, k, v = _in_proj(h, positions, inv_freq, row(norm_mix_g[i]), w_in[i])
        attn, (wout, wup, wdn, wg, wp) = _attn(
            q, k, v, bsz, seq,
            [w_out[i], w_ffn_up[i], w_ffn_down[i], w_ple_gate[i], w_ple_proj[i]])
        assert depth == 1
        h = _tail(u, h, attn, p[i].reshape(t, PLE_DIM), seq, conv_w[i], row(conv_b[i]),
                  row(conv_ln_g[i]), row(conv_ln_b[i]), wout,
                  row(norm_ffn_g[i]), wup, wdn, row(norm_ple_g[i]), wg, wp,
                  row(final_norm_g))
    return h.reshape(bsz, seq, D_MODEL)
```

```python
import functools

import jax
import jax.numpy as jnp
from jax import lax
from jax.experimental import pallas as pl
from jax.experimental.pallas import tpu as pltpu

D_MODEL = 1024
CONV_CH = 512
ATTN_HEADS = 8
HEAD_DIM = 64
ATTN_WIDTH = ATTN_HEADS * HEAD_DIM
IN_WIDTH = 2 * CONV_CH + 3 * ATTN_WIDTH
CONV_KERNEL = 31
MOBA_BLOCK = 256
MOBA_TOPK = 3
ROPE_THETA = 500000.0
ROPE_DIM = HEAD_DIM // 4
ROPE_HALF = ROPE_DIM // 2
D_FF = -(-8 * D_MODEL // (3 * 256)) * 256
PLE_DIM = 256
EPS = 1e-6

LANES = 128
SUBLANES = 8
BF16_ROWS = 16
LOG2E = 1.4426950408889634
HEADS_PER_STEP = LANES // HEAD_DIM
CONV_PAD = 32
CONV_ROWS = 32
IN_ROWS = 512
TAIL_ROWS = 256
VMEM_LIMIT = 56 * 1024 * 1024

F32 = jnp.float32
BF16 = jnp.bfloat16
NEG = float(jnp.finfo(jnp.float32).min)
POS = float(jnp.finfo(jnp.float32).max)


def _rmsnorm(x, g):
    return x * lax.rsqrt(jnp.mean(x * x, axis=-1, keepdims=True) + EPS) * g


def _dot(a, b):
    return jnp.dot(a, b, preferred_element_type=F32)


def _dot_nt(a, b):
    return lax.dot_general(a, b, (((1,), (1,)), ((), ())), preferred_element_type=F32)


def _rope_expand_matrix():
    r = jnp.arange(LANES)[:, None]
    l = jnp.arange(2 * LANES)[None, :]
    k = r % (2 * ROPE_HALF)
    piece_ok = r < 3 * 2 * ROPE_HALF
    d = (l % LANES) % HEAD_DIM
    want = d % ROPE_HALF + ROPE_HALF * (l // LANES)
    return (piece_ok & (d < ROPE_DIM) & (k == want)).astype(BF16)


def _in_proj_kernel(x_ref, pos_ref, freq_ref, expand_ref, g_ref, w32_ref,
                    u_ref, q_ref, k_ref, v_ref, w_ref):
    @pl.when(pl.program_id(0) == 0)
    def _():
        w_ref[...] = w32_ref[...].astype(BF16)

    rows = x_ref.shape[0]
    half = rows // 2
    zs = []
    for r in range(2):
        hn = _rmsnorm(x_ref[r * half:(r + 1) * half, :], g_ref[...]).astype(BF16)
        zs.append(_dot(hn, w_ref[...]))

    ang = pos_ref[0].astype(F32) * freq_ref[...]
    small = jnp.concatenate([jnp.cos(ang), jnp.sin(ang)], axis=0)
    hi = small.astype(BF16).astype(F32)
    mid = (small - hi).astype(BF16).astype(F32)
    lo = (small - hi - mid).astype(BF16).astype(F32)
    pad = jnp.zeros((LANES - 3 * small.shape[0], small.shape[1]), F32)
    pieces = jnp.concatenate([hi, mid, lo, pad], axis=0).T.astype(BF16)
    table = _dot(pieces, expand_ref[...])
    d = lax.broadcasted_iota(jnp.int32, (1, LANES), 1) & (HEAD_DIM - 1)
    cos = table[:, :LANES] + jnp.where(d < ROPE_DIM, 0.0, 1.0)
    sin = table[:, LANES:]
    sin_from_hi = jnp.where(d < ROPE_HALF, -sin, 0.0)
    sin_from_lo = jnp.where(d >= ROPE_HALF, sin, 0.0)

    q0 = 2 * CONV_CH
    k0 = q0 + ATTN_WIDTH
    v0 = k0 + ATTN_WIDTH
    scale = HEAD_DIM ** -0.5 * LOG2E
    for r, z in enumerate(zs):
        rs = slice(r * half, (r + 1) * half)
        u_ref[rs, :] = z[:, :CONV_CH] * jax.nn.sigmoid(z[:, CONV_CH:2 * CONV_CH])

        def rope(t, rs=rs):
            return (t * cos[rs]
                    + pltpu.roll(t, ROPE_HALF, 1) * sin_from_lo[rs]
                    + pltpu.roll(t, LANES - ROPE_HALF, 1) * sin_from_hi[rs])

        for c in range(ATTN_WIDTH // LANES):
            sl = slice(c * LANES, (c + 1) * LANES)
            q_ref[rs, sl] = (rope(z[:, q0 + c * LANES:q0 + (c + 1) * LANES]) * scale).astype(BF16)
            k_ref[rs, sl] = rope(z[:, k0 + c * LANES:k0 + (c + 1) * LANES]).astype(BF16)
        v_ref[rs, :] = z[:, v0:v0 + ATTN_WIDTH].astype(BF16)


def _in_proj(x2, positions, inv_freq, g, w):
    t = x2.shape[0]
    rows = IN_ROWS
    pos3 = positions.reshape(t // rows, 1, rows)
    freq = jnp.broadcast_to(inv_freq[:, None], (ROPE_HALF, rows))
    expand = _rope_expand_matrix()
    row_spec = lambda width: pl.BlockSpec((rows, width), lambda i: (i, 0))
    full = lambda a: pl.BlockSpec(a.shape, lambda i: (0, 0))
    return pl.pallas_call(
        _in_proj_kernel,
        grid=(t // rows,),
        in_specs=[row_spec(D_MODEL), pl.BlockSpec((1, 1, rows), lambda i: (i, 0, 0)),
                  full(freq), full(expand), full(g),
                  pl.BlockSpec(w.shape, lambda i: (0, 0), pipeline_mode=pl.Buffered(1))],
        out_specs=[row_spec(CONV_CH), row_spec(ATTN_WIDTH), row_spec(ATTN_WIDTH), row_spec(ATTN_WIDTH)],
        out_shape=[jax.ShapeDtypeStruct((t, CONV_CH), F32),
                   jax.ShapeDtypeStruct((t, ATTN_WIDTH), BF16),
                   jax.ShapeDtypeStruct((t, ATTN_WIDTH), BF16),
                   jax.ShapeDtypeStruct((t, ATTN_WIDTH), BF16)],
        scratch_shapes=[pltpu.VMEM(w.shape, BF16)],
        compiler_params=pltpu.CompilerParams(
            dimension_semantics=("arbitrary",), vmem_limit_bytes=VMEM_LIMIT),
        name="in_proj",
    )(x2, pos3, freq, expand, g, w)


def _conv_rows(window, w_ref, b_ref, lg_ref, lb_ref, not_before):
    first = CONV_PAD - (CONV_KERNEL - 1)
    win_rows = CONV_ROWS + CONV_PAD
    a = not_before[:BF16_ROWS, :LANES].astype(F32)[0:1, :]
    hold = a - a
    accs = []
    for g in range(CONV_CH // LANES):
        lanes = slice(g * LANES, (g + 1) * LANES)
        win = window(lanes)
        acc = jnp.broadcast_to(b_ref[:, lanes] + hold, (CONV_ROWS, LANES))
        for phase in range(SUBLANES):
            shifted = win if phase == 0 else pltpu.roll(win, win_rows - phase, 0)
            for base in range(0, CONV_PAD + 1, SUBLANES):
                j = base + phase - first
                if 0 <= j < CONV_KERNEL:
                    acc = acc + w_ref[j:j + 1, lanes] * shifted[base:base + CONV_ROWS]
        accs.append(acc)
        hold = acc[0:1, :] - acc[0:1, :]
    acc = jnp.concatenate(accs, axis=1)
    mu = jnp.mean(acc, axis=-1, keepdims=True)
    dev = acc - mu
    var = jnp.mean(dev * dev, axis=-1, keepdims=True)
    y = dev * lax.rsqrt(var + EPS) * lg_ref[...] + lb_ref[...]
    return (y * jax.nn.sigmoid(y)).astype(BF16)


def _attn_kernel(n_weights, q_ref, k_ref, v_ref, *refs):
    w32_refs = refs[:n_weights]
    o_ref = refs[n_weights]
    w16_refs = refs[n_weights + 1:2 * n_weights + 1]
    ot_ref = refs[2 * n_weights + 1]
    for src, dst in zip(w32_refs, w16_refs):
        dst[...] = src[...].astype(BF16)

    seq = q_ref.shape[0]
    nb = seq // MOBA_BLOCK
    q = q_ref[...]
    k = k_ref[...]
    vt = v_ref[...].astype(F32).T.astype(BF16)
    lane = lax.broadcasted_iota(jnp.int32, (1, LANES), 1)
    kmean = jnp.mean(k.astype(F32).reshape(nb, MOBA_BLOCK, LANES), axis=1)
    heads = HEADS_PER_STEP
    cols = heads * MOBA_BLOCK
    key_row = lax.broadcasted_iota(jnp.int32, (MOBA_BLOCK, cols), 0)
    q_col = lax.broadcasted_iota(jnp.int32, (MOBA_BLOCK, cols), 1) & (MOBA_BLOCK - 1)
    causal = key_row <= q_col
    blk = lax.broadcasted_iota(jnp.int32, (nb, MOBA_BLOCK), 0)
    first_ranked = MOBA_TOPK + 1
    vt_aug = jnp.concatenate([vt, jnp.ones((BF16_ROWS, seq), BF16)], axis=0)

    in_head = [(lane >= h * HEAD_DIM) & (lane < (h + 1) * HEAD_DIM) for h in range(heads)]
    gate_t = [_dot_nt(jnp.where(m, kmean, 0.0).astype(BF16), q[first_ranked * MOBA_BLOCK:])
              for m in in_head]

    def scores(i):
        qi = q[i * MOBA_BLOCK:(i + 1) * MOBA_BLOCK]
        q2 = jnp.concatenate([jnp.where(m, qi, jnp.zeros_like(qi)) for m in in_head], axis=0)
        return _dot_nt(k[:(i + 1) * MOBA_BLOCK], q2)

    def finish(i, st):
        if i >= first_ranked:
            caps = []
            for h in range(heads):
                c0 = (i - first_ranked) * MOBA_BLOCK
                gi = gate_t[h][:, c0:c0 + MOBA_BLOCK]
                rank = jnp.zeros((nb, MOBA_BLOCK), jnp.int32)
                for m in range(i):
                    gm = gi[m:m + 1, :]
                    beats = (gm > gi) | ((gm == gi) & (blk > m))
                    rank = rank + jnp.where(beats, 1, 0)
                caps.append(jnp.where(rank < MOBA_TOPK, POS, NEG))
            cap = jnp.concatenate(caps, axis=1)
        pieces = []
        for j in range(i + 1):
            sj = st[j * MOBA_BLOCK:(j + 1) * MOBA_BLOCK]
            if j == i:
                sj = jnp.where(causal, sj, NEG)
            elif i >= first_ranked:
                sj = jnp.minimum(sj, cap[j:j + 1, :])
            pieces.append(sj)
        st = pieces[0] if i == 0 else jnp.concatenate(pieces, axis=0)
        m_col = jnp.max(st, axis=0, keepdims=True)
        p = jnp.exp2(st - m_col).astype(BF16)
        ot = _dot(vt_aug[:, :(i + 1) * MOBA_BLOCK], p)
        denom = ot[LANES:LANES + 1]
        for h in range(heads):
            rows = slice(h * HEAD_DIM, (h + 1) * HEAD_DIM)
            qcols = slice(h * MOBA_BLOCK, (h + 1) * MOBA_BLOCK)
            ot_ref[rows, i * MOBA_BLOCK:(i + 1) * MOBA_BLOCK] = ot[rows, qcols] / denom[:, qcols]

    st_next = scores(0)
    for i in range(nb):
        st = st_next
        if i + 1 < nb:
            st_next = scores(i + 1)
        finish(i, st)
    o_ref[...] = ot_ref[...].T.astype(BF16)


def _attn(q2, k2, v2, bsz, seq, weights):
    pairs = ATTN_WIDTH // LANES
    n_steps = bsz * pairs
    spec = pl.BlockSpec((seq, LANES), lambda b, hp: (b, hp))

    def slice_spec(w):
        n = next(n for n in (n_steps, n_steps // 2, n_steps // 4, 1)
                 if w.shape[0] % n == 0 and (w.shape[0] // n) % BF16_ROWS == 0)
        return pl.BlockSpec((w.shape[0] // n, w.shape[1]),
                            lambda b, hp: (jnp.minimum(b * pairs + hp, n - 1), 0))

    w_specs = [slice_spec(w) for w in weights]
    out = pl.pallas_call(
        functools.partial(_attn_kernel, len(weights)),
        grid=(bsz, pairs),
        in_specs=[spec, spec, spec] + w_specs,
        out_specs=[spec] + w_specs,
        out_shape=[jax.ShapeDtypeStruct(q2.shape, BF16)]
        + [jax.ShapeDtypeStruct(w.shape, BF16) for w in weights],
        scratch_shapes=[pltpu.VMEM((LANES, seq), F32)],
        compiler_params=pltpu.CompilerParams(
            dimension_semantics=("arbitrary", "arbitrary"), vmem_limit_bytes=VMEM_LIMIT),
        name="attn",
    )(q2, k2, v2, *weights)
    return out[0], out[1:]


def _ordered_after(x, anchor):
    rows, width = anchor.shape
    parts = [anchor[r:r + BF16_ROWS, c:c + LANES]
             for r in range(0, rows, BF16_ROWS) for c in range(0, width, LANES)]
    while len(parts) > 1:
        parts = [jnp.maximum(parts[n], parts[n + 1]) for n in range(0, len(parts), 2)]
    a = parts[0]
    head = x[:BF16_ROWS, :LANES] + (a - a)
    top = jnp.concatenate([head, x[:BF16_ROWS, LANES:]], axis=1)
    return jnp.concatenate([top, x[BF16_ROWS:]], axis=0)


def _tail_kernel(tiles_per_seq, u_ref, halo_ref, x_ref, a_ref, p_ref, cw_ref, cb_ref, lg_ref, lb_ref,
                 wout_ref, gf_ref, wup_ref, wdn_ref, gp_ref, wg_ref, wp_ref, gfin_ref,
                 o_ref, conv_ref):
    i = pl.program_id(0)
    n_tiles = pl.num_programs(0) - 1

    @pl.when(i == 0)
    def _():
        conv_ref[...] = jnp.zeros_like(conv_ref)

    tile = jnp.minimum(i, n_tiles - 1)
    starts_seq = lax.rem(tile, tiles_per_seq) == 0
    halo = jnp.where(starts_seq, 0.0, halo_ref[...])

    def conv_chunk(c, not_before):
        def window(lanes):
            start = c * CONV_ROWS - CONV_PAD
            stop = (c + 1) * CONV_ROWS
            if start < 0:
                return jnp.concatenate([halo[CONV_PAD + start:, lanes], u_ref[0:stop, lanes]], axis=0)
            return u_ref[start:stop, lanes]
        y = _conv_rows(window, cw_ref, cb_ref, lg_ref, lb_ref, not_before)
        conv_ref[c * CONV_ROWS:(c + 1) * CONV_ROWS, :] = y
        return y

    h = (x_ref[...]
         + _dot(conv_ref[...], wout_ref[:CONV_CH, :])
         + _dot(a_ref[...], wout_ref[CONV_CH:, :]))
    hn = _rmsnorm(h, gf_ref[...]).astype(BF16)
    n_chunks = u_ref.shape[0] // CONV_ROWS
    bounds = [0, (4 * n_chunks) // 8, (7 * n_chunks) // 8, n_chunks]

    def conv_group(q, x, not_before):
        for c in range(bounds[q], bounds[q + 1]):
            y = conv_chunk(c, not_before)
            x = _ordered_after(x, y)
            not_before = y
        return x

    gt = _dot(hn, wup_ref[:, :D_FF])
    hn = conv_group(0, hn, hn)
    up = _dot(hn, wup_ref[:, D_FF:])
    act = conv_group(1, (jax.nn.silu(gt) * up).astype(BF16), hn)
    h = h + _dot(act, wdn_ref[...])
    hn = conv_group(2, _rmsnorm(h, gp_ref[...]).astype(BF16), act)
    gate = jax.nn.sigmoid(_dot(hn, wg_ref[...]))
    h = h + gate * _dot(p_ref[...].astype(BF16), wp_ref[...])
    o_ref[...] = _rmsnorm(h, gfin_ref[...])


def _tail(u2, x2, attn2, p2, seq, cw, cb, lg, lb, wout, gf, wup, wdn, gp, wg, wp, gfin):
    t = x2.shape[0]
    rows = TAIL_ROWS
    n_tiles = t // rows
    halo_per_tile = rows // CONV_PAD
    cur = lambda i: jnp.minimum(i, n_tiles - 1)
    prev = lambda i: jnp.maximum(i - 1, 0)
    prev_spec = lambda width: pl.BlockSpec((rows, width), lambda i: (prev(i), 0))
    full = lambda a: pl.BlockSpec(a.shape, lambda i: (0, 0), pipeline_mode=pl.Buffered(1))
    return pl.pallas_call(
        functools.partial(_tail_kernel, seq // rows),
        grid=(n_tiles + 1,),
        in_specs=[pl.BlockSpec((rows, CONV_CH), lambda i: (cur(i), 0)),
                  pl.BlockSpec((CONV_PAD, CONV_CH),
                               lambda i: (jnp.maximum(cur(i) * halo_per_tile - 1, 0), 0)),
                  prev_spec(D_MODEL), prev_spec(ATTN_WIDTH), prev_spec(PLE_DIM),
                  full(cw), full(cb), full(lg), full(lb),
                  full(wout), full(gf), full(wup), full(wdn), full(gp), full(wg), full(wp),
                  full(gfin)],
        out_specs=prev_spec(D_MODEL),
        out_shape=jax.ShapeDtypeStruct((t, D_MODEL), F32),
        scratch_shapes=[pltpu.VMEM((rows, CONV_CH), BF16)],
        compiler_params=pltpu.CompilerParams(
            dimension_semantics=("arbitrary",), vmem_limit_bytes=VMEM_LIMIT),
        name="tail",
    )(u2, u2, x2, attn2, p2, cw, cb, lg, lb, wout, gf, wup, wdn, gp, wg, wp, gfin)


def kernel(x, p, positions, norm_mix_g, w_in, conv_w, conv_b, conv_ln_g, conv_ln_b, w_out,
           norm_ffn_g, w_ffn_up, w_ffn_down, norm_ple_g, w_ple_gate, w_ple_proj, final_norm_g):
    bsz, seq, _ = x.shape
    t = bsz * seq
    depth = w_in.shape[0]
    row = lambda a: a.reshape(1, -1)
    inv_freq = ROPE_THETA ** (-jnp.arange(0, ROPE_DIM, 2, dtype=F32) / ROPE_DIM)
    h = x.reshape(t, D_MODEL)
    for i in range(depth):
        u, q, k, v = _in_proj(h, positions, inv_freq, row(norm_mix_g[i]), w_in[i])
        attn, (wout, wup, wdn, wg, wp) = _attn(
            q, k, v, bsz, seq,
            [w_out[i], w_ffn_up[i], w_ffn_down[i], w_ple_gate[i], w_ple_proj[i]])
        assert depth == 1
        h = _tail(u, h, attn, p[i].reshape(t, PLE_DIM), seq, conv_w[i], row(conv_b[i]),
                  row(conv_ln_g[i]), row(conv_ln_b[i]), wout,
                  row(norm_ffn_g[i]), wup, wdn, row(norm_ple_g[i]), wg, wp,
                  row(final_norm_g))
    return h.reshape(bsz, seq, D_MODEL)
```

```python
import functools

import jax
import jax.numpy as jnp
from jax import lax
from jax.experimental import pallas as pl
from jax.experimental.pallas import tpu as pltpu

D_MODEL = 1024
CONV_CH = 512
ATTN_HEADS = 8
HEAD_DIM = 64
ATTN_WIDTH = ATTN_HEADS * HEAD_DIM
IN_WIDTH = 2 * CONV_CH + 3 * ATTN_WIDTH
CONV_KERNEL = 31
MOBA_BLOCK = 256
MOBA_TOPK = 3
ROPE_THETA = 500000.0
ROPE_DIM = HEAD_DIM // 4
ROPE_HALF = ROPE_DIM // 2
D_FF = -(-8 * D_MODEL // (3 * 256)) * 256
PLE_DIM = 256
EPS = 1e-6

LANES = 128
SUBLANES = 8
BF16_ROWS = 16
LOG2E = 1.4426950408889634
HEADS_PER_STEP = LANES // HEAD_DIM
CONV_PAD = 32
CONV_ROWS = 32
IN_ROWS = 512
TAIL_ROWS = 512
VMEM_LIMIT = 56 * 1024 * 1024

F32 = jnp.float32
BF16 = jnp.bfloat16
NEG = float(jnp.finfo(jnp.float32).min)
POS = float(jnp.finfo(jnp.float32).max)


def _rmsnorm(x, g):
    return x * lax.rsqrt(jnp.mean(x * x, axis=-1, keepdims=True) + EPS) * g


def _dot(a, b):
    return jnp.dot(a, b, preferred_element_type=F32)


def _dot_nt(a, b):
    return lax.dot_general(a, b, (((1,), (1,)), ((), ())), preferred_element_type=F32)


def _rope_expand_matrix():
    r = jnp.arange(LANES)[:, None]
    l = jnp.arange(2 * LANES)[None, :]
    k = r % (2 * ROPE_HALF)
    piece_ok = r < 3 * 2 * ROPE_HALF
    d = (l % LANES) % HEAD_DIM
    want = d % ROPE_HALF + ROPE_HALF * (l // LANES)
    return (piece_ok & (d < ROPE_DIM) & (k == want)).astype(BF16)


def _in_proj_kernel(x_ref, pos_ref, freq_ref, expand_ref, g_ref, w32_ref,
                    u_ref, q_ref, k_ref, v_ref, w_ref):
    @pl.when(pl.program_id(0) == 0)
    def _():
        w_ref[...] = w32_ref[...].astype(BF16)

    rows = x_ref.shape[0]
    half = rows // 2

    ang = pos_ref[0].astype(F32) * freq_ref[...]
    small = jnp.concatenate([jnp.cos(ang), jnp.sin(ang)], axis=0)
    hi = small.astype(BF16).astype(F32)
    mid = (small - hi).astype(BF16).astype(F32)
    lo = (small - hi - mid).astype(BF16).astype(F32)
    pad = jnp.zeros((LANES - 3 * small.shape[0], small.shape[1]), F32)
    pieces = jnp.concatenate([hi, mid, lo, pad], axis=0).T.astype(BF16)
    table = _dot(pieces, expand_ref[...])
    d = lax.broadcasted_iota(jnp.int32, (1, LANES), 1) & (HEAD_DIM - 1)
    cos = table[:, :LANES] + jnp.where(d < ROPE_DIM, 0.0, 1.0)
    sin = table[:, LANES:]
    sin_from_hi = jnp.where(d < ROPE_HALF, -sin, 0.0)
    sin_from_lo = jnp.where(d >= ROPE_HALF, sin, 0.0)

    q0 = 2 * CONV_CH
    k0 = q0 + ATTN_WIDTH
    v0 = k0 + ATTN_WIDTH
    scale = HEAD_DIM ** -0.5 * LOG2E

    def epilogue(r, z_u, z_qkv):
        rs = slice(r * half, (r + 1) * half)
        u = z_u[:, :CONV_CH] * jax.nn.sigmoid(z_u[:, CONV_CH:])
        u_ref[rs, :] = u
        made = [u]

        def rope(t):
            return (t * cos[rs]
                    + pltpu.roll(t, ROPE_HALF, 1) * sin_from_lo[rs]
                    + pltpu.roll(t, LANES - ROPE_HALF, 1) * sin_from_hi[rs])

        for c in range(ATTN_WIDTH // LANES):
            sl = slice(c * LANES, (c + 1) * LANES)
            qc = rope(z_qkv[:, c * LANES:(c + 1) * LANES]) * scale
            kc = rope(z_qkv[:, ATTN_WIDTH + c * LANES:ATTN_WIDTH + (c + 1) * LANES])
            q_ref[rs, sl] = qc.astype(BF16)
            k_ref[rs, sl] = kc.astype(BF16)
            made += [qc, kc]
        v_ref[rs, :] = z_qkv[:, 2 * ATTN_WIDTH:].astype(BF16)
        return made

    norm = lambda r: _rmsnorm(x_ref[r * half:(r + 1) * half, :], g_ref[...]).astype(BF16)
    z_a = _dot(norm(0), w_ref[...])
    hn_b = norm(1)
    z_b_u = _dot(hn_b, w_ref[:, :q0])
    made = epilogue(0, z_a[:, :q0], z_a[:, q0:])
    parts = [m[r:r + SUBLANES, c:c + LANES] for m in made
             for r in range(0, half, SUBLANES) for c in range(0, m.shape[1], LANES)]
    while len(parts) > 1:
        pairs = [jnp.maximum(parts[n], parts[n + 1]) for n in range(0, len(parts) - 1, 2)]
        parts = pairs + parts[len(parts) - len(parts) % 2:]
    zero = parts[0] - parts[0]
    head = hn_b[:BF16_ROWS, :LANES] + jnp.concatenate([zero, zero], axis=0).astype(BF16)
    top = jnp.concatenate([head, hn_b[:BF16_ROWS, LANES:]], axis=1)
    hn_b = jnp.concatenate([top, hn_b[BF16_ROWS:]], axis=0)
    z_b_qkv = _dot(hn_b, w_ref[:, q0:])
    epilogue(1, z_b_u, z_b_qkv)


def _in_proj(x2, positions, inv_freq, g, w):
    t = x2.shape[0]
    rows = IN_ROWS
    pos3 = positions.reshape(t // rows, 1, rows)
    freq = jnp.broadcast_to(inv_freq[:, None], (ROPE_HALF, rows))
    expand = _rope_expand_matrix()
    row_spec = lambda width: pl.BlockSpec((rows, width), lambda i: (i, 0))
    full = lambda a: pl.BlockSpec(a.shape, lambda i: (0, 0))
    return pl.pallas_call(
        _in_proj_kernel,
        grid=(t // rows,),
        in_specs=[row_spec(D_MODEL), pl.BlockSpec((1, 1, rows), lambda i: (i, 0, 0)),
                  full(freq), full(expand), full(g),
                  pl.BlockSpec(w.shape, lambda i: (0, 0), pipeline_mode=pl.Buffered(1))],
        out_specs=[row_spec(CONV_CH), row_spec(ATTN_WIDTH), row_spec(ATTN_WIDTH), row_spec(ATTN_WIDTH)],
        out_shape=[jax.ShapeDtypeStruct((t, CONV_CH), F32),
                   jax.ShapeDtypeStruct((t, ATTN_WIDTH), BF16),
                   jax.ShapeDtypeStruct((t, ATTN_WIDTH), BF16),
                   jax.ShapeDtypeStruct((t, ATTN_WIDTH), BF16)],
        scratch_shapes=[pltpu.VMEM(w.shape, BF16)],
        compiler_params=pltpu.CompilerParams(
            dimension_semantics=("arbitrary",), vmem_limit_bytes=VMEM_LIMIT),
        name="in_proj",
    )(x2, pos3, freq, expand, g, w)


def _conv_rows(window, w_ref, b_ref, lg_ref, lb_ref, not_before):
    first = CONV_PAD - (CONV_KERNEL - 1)
    win_rows = CONV_ROWS + CONV_PAD
    a = not_before[:BF16_ROWS, :LANES].astype(F32)[0:1, :]
    hold = a - a
    accs = []
    for g in range(CONV_CH // LANES):
        lanes = slice(g * LANES, (g + 1) * LANES)
        win = window(lanes)
        acc = jnp.broadcast_to(b_ref[:, lanes] + hold, (CONV_ROWS, LANES))
        for phase in range(SUBLANES):
            shifted = win if phase == 0 else pltpu.roll(win, win_rows - phase, 0)
            for base in range(0, CONV_PAD + 1, SUBLANES):
                j = base + phase - first
                if 0 <= j < CONV_KERNEL:
                    acc = acc + w_ref[j:j + 1, lanes] * shifted[base:base + CONV_ROWS]
        accs.append(acc)
        hold = acc[0:1, :] - acc[0:1, :]
    acc = jnp.concatenate(accs, axis=1)
    mu = jnp.mean(acc, axis=-1, keepdims=True)
    dev = acc - mu
    var = jnp.mean(dev * dev, axis=-1, keepdims=True)
    y = dev * lax.rsqrt(var + EPS) * lg_ref[...] + lb_ref[...]
    return (y * jax.nn.sigmoid(y)).astype(BF16)


def _attn_kernel(n_weights, q_ref, k_ref, v_ref, *refs):
    w32_refs = refs[:n_weights]
    o_ref = refs[n_weights]
    w16_refs = refs[n_weights + 1:2 * n_weights + 1]
    ot_ref = refs[2 * n_weights + 1]
    for src, dst in zip(w32_refs, w16_refs):
        dst[...] = src[...].astype(BF16)

    seq = q_ref.shape[0]
    nb = seq // MOBA_BLOCK
    q = q_ref[...]
    k = k_ref[...]
    vt = v_ref[...].astype(F32).T.astype(BF16)
    lane = lax.broadcasted_iota(jnp.int32, (1, LANES), 1)
    kmean = jnp.mean(k.astype(F32).reshape(nb, MOBA_BLOCK, LANES), axis=1)
    heads = HEADS_PER_STEP
    cols = heads * MOBA_BLOCK
    key_row = lax.broadcasted_iota(jnp.int32, (MOBA_BLOCK, cols), 0)
    q_col = lax.broadcasted_iota(jnp.int32, (MOBA_BLOCK, cols), 1) & (MOBA_BLOCK - 1)
    causal = key_row <= q_col
    blk = lax.broadcasted_iota(jnp.int32, (nb, MOBA_BLOCK), 0)
    first_ranked = MOBA_TOPK + 1
    vt_aug = jnp.concatenate([vt, jnp.ones((BF16_ROWS, seq), BF16)], axis=0)

    in_head = [(lane >= h * HEAD_DIM) & (lane < (h + 1) * HEAD_DIM) for h in range(heads)]
    gate_t = [_dot_nt(jnp.where(m, kmean, 0.0).astype(BF16), q[first_ranked * MOBA_BLOCK:])
              for m in in_head]

    def scores(i):
        qi = q[i * MOBA_BLOCK:(i + 1) * MOBA_BLOCK]
        q2 = jnp.concatenate([jnp.where(m, qi, jnp.zeros_like(qi)) for m in in_head], axis=0)
        return _dot_nt(k[:(i + 1) * MOBA_BLOCK], q2)

    def finish(i, st):
        if i >= first_ranked:
            caps = []
            for h in range(heads):
                c0 = (i - first_ranked) * MOBA_BLOCK
                gi = gate_t[h][:, c0:c0 + MOBA_BLOCK]
                rank = jnp.zeros((nb, MOBA_BLOCK), jnp.int32)
                for m in range(i):
                    gm = gi[m:m + 1, :]
                    beats = (gm > gi) | ((gm == gi) & (blk > m))
                    rank = rank + jnp.where(beats, 1, 0)
                caps.append(jnp.where(rank < MOBA_TOPK, POS, NEG))
            cap = jnp.concatenate(caps, axis=1)
        pieces = []
        for j in range(i + 1):
            sj = st[j * MOBA_BLOCK:(j + 1) * MOBA_BLOCK]
            if j == i:
                sj = jnp.where(causal, sj, NEG)
            elif i >= first_ranked:
                sj = jnp.minimum(sj, cap[j:j + 1, :])
            pieces.append(sj)
        st = pieces[0] if i == 0 else jnp.concatenate(pieces, axis=0)
        m_col = jnp.max(st, axis=0, keepdims=True)
        p = jnp.exp2(st - m_col).astype(BF16)
        ot = _dot(vt_aug[:, :(i + 1) * MOBA_BLOCK], p)
        denom = ot[LANES:LANES + 1]
        for h in range(heads):
            rows = slice(h * HEAD_DIM, (h + 1) * HEAD_DIM)
            qcols = slice(h * MOBA_BLOCK, (h + 1) * MOBA_BLOCK)
            ot_ref[rows, i * MOBA_BLOCK:(i + 1) * MOBA_BLOCK] = ot[rows, qcols] / denom[:, qcols]

    st_next = scores(0)
    for i in range(nb):
        st = st_next
        if i + 1 < nb:
            st_next = scores(i + 1)
        finish(i, st)
    o_ref[...] = ot_ref[...].T.astype(BF16)


def _attn(q2, k2, v2, bsz, seq, weights):
    pairs = ATTN_WIDTH // LANES
    n_steps = bsz * pairs
    spec = pl.BlockSpec((seq, LANES), lambda b, hp: (b, hp))

    def slice_spec(w):
        n = next(n for n in (n_steps, n_steps // 2, n_steps // 4, 1)
                 if w.shape[0] % n == 0 and (w.shape[0] // n) % BF16_ROWS == 0)
        return pl.BlockSpec((w.shape[0] // n, w.shape[1]),
                            lambda b, hp: (jnp.minimum(b * pairs + hp, n - 1), 0))

    w_specs = [slice_spec(w) for w in weights]
    out = pl.pallas_call(
        functools.partial(_attn_kernel, len(weights)),
        grid=(bsz, pairs),
        in_specs=[spec, spec, spec] + w_specs,
        out_specs=[spec] + w_specs,
        out_shape=[jax.ShapeDtypeStruct(q2.shape, BF16)]
        + [jax.ShapeDtypeStruct(w.shape, BF16) for w in weights],
        scratch_shapes=[pltpu.VMEM((LANES, seq), F32)],
        compiler_params=pltpu.CompilerParams(
            dimension_semantics=("arbitrary", "arbitrary"), vmem_limit_bytes=VMEM_LIMIT),
        name="attn",
    )(q2, k2, v2, *weights)
    return out[0], out[1:]


def _ordered_after(x, anchor):
    rows, width = anchor.shape
    parts = [anchor[r:r + BF16_ROWS, c:c + LANES]
             for r in range(0, rows, BF16_ROWS) for c in range(0, width, LANES)]
    while len(parts) > 1:
        parts = [jnp.maximum(parts[n], parts[n + 1]) for n in range(0, len(parts), 2)]
    a = parts[0]
    head = x[:BF16_ROWS, :LANES] + (a - a)
    top = jnp.concatenate([head, x[:BF16_ROWS, LANES:]], axis=1)
    return jnp.concatenate([top, x[BF16_ROWS:]], axis=0)


def _tail_kernel(tiles_per_seq, u_ref, halo_ref, x_ref, a_ref, p_ref, cw_ref, cb_ref, lg_ref, lb_ref,
                 wout_ref, gf_ref, wup_ref, wdn_ref, gp_ref, wg_ref, wp_ref, gfin_ref,
                 o_ref, conv_ref):
    i = pl.program_id(0)
    n_tiles = pl.num_programs(0) - 1

    @pl.when(i == 0)
    def _():
        conv_ref[...] = jnp.zeros_like(conv_ref)

    tile = jnp.minimum(i, n_tiles - 1)
    starts_seq = lax.rem(tile, tiles_per_seq) == 0
    halo = jnp.where(starts_seq, 0.0, halo_ref[...])

    def conv_chunk(c, not_before):
        def window(lanes):
            start = c * CONV_ROWS - CONV_PAD
            stop = (c + 1) * CONV_ROWS
            if start < 0:
                return jnp.concatenate([halo[CONV_PAD + start:, lanes], u_ref[0:stop, lanes]], axis=0)
            return u_ref[start:stop, lanes]
        y = _conv_rows(window, cw_ref, cb_ref, lg_ref, lb_ref, not_before)
        conv_ref[c * CONV_ROWS:(c + 1) * CONV_ROWS, :] = y
        return y

    h = (x_ref[...]
         + _dot(conv_ref[...], wout_ref[:CONV_CH, :])
         + _dot(a_ref[...], wout_ref[CONV_CH:, :]))
    hn = _rmsnorm(h, gf_ref[...]).astype(BF16)
    n_chunks = u_ref.shape[0] // CONV_ROWS
    bounds = [0, (4 * n_chunks) // 8, (7 * n_chunks) // 8, n_chunks]

    def conv_group(q, x, not_before):
        for c in range(bounds[q], bounds[q + 1]):
            y = conv_chunk(c, not_before)
            x = _ordered_after(x, y)
            not_before = y
        return x

    gt = _dot(hn, wup_ref[:, :D_FF])
    hn = conv_group(0, hn, hn)
    up = _dot(hn, wup_ref[:, D_FF:])
    act = conv_group(1, (jax.nn.silu(gt) * up).astype(BF16), hn)
    h = h + _dot(act, wdn_ref[...])
    hn = conv_group(2, _rmsnorm(h, gp_ref[...]).astype(BF16), act)
    gate = jax.nn.sigmoid(_dot(hn, wg_ref[...]))
    h = h + gate * _dot(p_ref[...].astype(BF16), wp_ref[...])
    o_ref[...] = _rmsnorm(h, gfin_ref[...])


def _tail(u2, x2, attn2, p2, seq, cw, cb, lg, lb, wout, gf, wup, wdn, gp, wg, wp, gfin):
    t = x2.shape[0]
    rows = TAIL_ROWS
    n_tiles = t // rows
    halo_per_tile = rows // CONV_PAD
    cur = lambda i: jnp.minimum(i, n_tiles - 1)
    prev = lambda i: jnp.maximum(i - 1, 0)
    prev_spec = lambda width: pl.BlockSpec((rows, width), lambda i: (prev(i), 0))
    full = lambda a: pl.BlockSpec(a.shape, lambda i: (0, 0), pipeline_mode=pl.Buffered(1))
    return pl.pallas_call(
        functools.partial(_tail_kernel, seq // rows),
        grid=(n_tiles + 1,),
        in_specs=[pl.BlockSpec((rows, CONV_CH), lambda i: (cur(i), 0)),
                  pl.BlockSpec((CONV_PAD, CONV_CH),
                               lambda i: (jnp.maximum(cur(i) * halo_per_tile - 1, 0), 0)),
                  prev_spec(D_MODEL), prev_spec(ATTN_WIDTH), prev_spec(PLE_DIM),
                  full(cw), full(cb), full(lg), full(lb),
                  full(wout), full(gf), full(wup), full(wdn), full(gp), full(wg), full(wp),
                  full(gfin)],
        out_specs=prev_spec(D_MODEL),
        out_shape=jax.ShapeDtypeStruct((t, D_MODEL), F32),
        scratch_shapes=[pltpu.VMEM((rows, CONV_CH), BF16)],
        compiler_params=pltpu.CompilerParams(
            dimension_semantics=("arbitrary",), vmem_limit_bytes=VMEM_LIMIT),
        name="tail",
    )(u2, u2, x2, attn2, p2, cw, cb, lg, lb, wout, gf, wup, wdn, gp, wg, wp, gfin)


def kernel(x, p, positions, norm_mix_g, w_in, conv_w, conv_b, conv_ln_g, conv_ln_b, w_out,
           norm_ffn_g, w_ffn_up, w_ffn_down, norm_ple_g, w_ple_gate, w_ple_proj, final_norm_g):
    bsz, seq, _ = x.shape
    t = bsz * seq
    depth = w_in.shape[0]
    row = lambda a: a.reshape(1, -1)
    inv_freq = ROPE_THETA ** (-jnp.arange(0, ROPE_DIM, 2, dtype=F32) / ROPE_DIM)
    h = x.reshape(t, D_MODEL)
    for i in range(depth):
        u, q, k, v = _in_proj(h, positions, inv_freq, row(norm_mix_g[i]), w_in[i])
        attn, (wout, wup, wdn, wg, wp) = _attn(
            q, k, v, bsz, seq,
            [w_out[i], w_ffn_up[i], w_ffn_down[i], w_ple_gate[i], w_ple_proj[i]])
        assert depth == 1
        h = _tail(u, h, attn, p[i].reshape(t, PLE_DIM), seq, conv_w[i], row(conv_b[i]),
                  row(conv_ln_g[i]), row(conv_ln_b[i]), wout,
                  row(norm_ffn_g[i]), wup, wdn, row(norm_ple_g[i]), wg, wp,
                  row(final_norm_g))
    return h.reshape(bsz, seq, D_MODEL)
```

```python
import functools

import jax
import jax.numpy as jnp
from jax import lax
from jax.experimental import pallas as pl
from jax.experimental.pallas import tpu as pltpu

D_MODEL = 1024
CONV_CH = 512
ATTN_HEADS = 8
HEAD_DIM = 64
ATTN_WIDTH = ATTN_HEADS * HEAD_DIM
IN_WIDTH = 2 * CONV_CH + 3 * ATTN_WIDTH
CONV_KERNEL = 31
MOBA_BLOCK = 256
MOBA_TOPK = 3
ROPE_THETA = 500000.0
ROPE_DIM = HEAD_DIM // 4
ROPE_HALF = ROPE_DIM // 2
D_FF = -(-8 * D_MODEL // (3 * 256)) * 256
PLE_DIM = 256
EPS = 1e-6

LANES = 128
SUBLANES = 8
BF16_ROWS = 16
LOG2E = 1.4426950408889634
HEADS_PER_STEP = LANES // HEAD_DIM
CONV_PAD = 32
CONV_ROWS = 32
IN_ROWS = 512
TAIL_ROWS = 512
VMEM_LIMIT = 56 * 1024 * 1024

F32 = jnp.float32
BF16 = jnp.bfloat16
NEG = float(jnp.finfo(jnp.float32).min)
POS = float(jnp.finfo(jnp.float32).max)


def _rmsnorm(x, g):
    return x * lax.rsqrt(jnp.mean(x * x, axis=-1, keepdims=True) + EPS) * g


def _dot(a, b):
    return jnp.dot(a, b, preferred_element_type=F32)


def _dot_nt(a, b):
    return lax.dot_general(a, b, (((1,), (1,)), ((), ())), preferred_element_type=F32)


def _rope_expand_matrix():
    r = jnp.arange(LANES)[:, None]
    l = jnp.arange(2 * LANES)[None, :]
    k = r % (2 * ROPE_HALF)
    piece_ok = r < 3 * 2 * ROPE_HALF
    d = (l % LANES) % HEAD_DIM
    want = d % ROPE_HALF + ROPE_HALF * (l // LANES)
    return (piece_ok & (d < ROPE_DIM) & (k == want)).astype(BF16)


def _in_proj_kernel(x_ref, pos_ref, freq_ref, expand_ref, g_ref, w32_ref,
                    u_ref, q_ref, k_ref, v_ref, w_ref):
    @pl.when(pl.program_id(0) == 0)
    def _():
        w_ref[...] = w32_ref[...].astype(BF16)

    rows = x_ref.shape[0]
    half = rows // 2

    ang = pos_ref[0].astype(F32) * freq_ref[...]
    small = jnp.concatenate([jnp.cos(ang), jnp.sin(ang)], axis=0)
    hi = small.astype(BF16).astype(F32)
    mid = (small - hi).astype(BF16).astype(F32)
    lo = (small - hi - mid).astype(BF16).astype(F32)
    pad = jnp.zeros((LANES - 3 * small.shape[0], small.shape[1]), F32)
    pieces = jnp.concatenate([hi, mid, lo, pad], axis=0).T.astype(BF16)
    table = _dot(pieces, expand_ref[...])
    d = lax.broadcasted_iota(jnp.int32, (1, LANES), 1) & (HEAD_DIM - 1)
    cos = table[:, :LANES] + jnp.where(d < ROPE_DIM, 0.0, 1.0)
    sin = table[:, LANES:]
    sin_from_hi = jnp.where(d < ROPE_HALF, -sin, 0.0)
    sin_from_lo = jnp.where(d >= ROPE_HALF, sin, 0.0)

    q0 = 2 * CONV_CH
    k0 = q0 + ATTN_WIDTH
    v0 = k0 + ATTN_WIDTH
    scale = HEAD_DIM ** -0.5 * LOG2E

    def epilogue(r, z_u, z_qkv):
        rs = slice(r * half, (r + 1) * half)
        u = z_u[:, :CONV_CH] * jax.nn.sigmoid(z_u[:, CONV_CH:])
        u_ref[rs, :] = u
        made = [u]

        def rope(t):
            return (t * cos[rs]
                    + pltpu.roll(t, ROPE_HALF, 1) * sin_from_lo[rs]
                    + pltpu.roll(t, LANES - ROPE_HALF, 1) * sin_from_hi[rs])

        for c in range(ATTN_WIDTH // LANES):
            sl = slice(c * LANES, (c + 1) * LANES)
            qc = rope(z_qkv[:, c * LANES:(c + 1) * LANES]) * scale
            kc = rope(z_qkv[:, ATTN_WIDTH + c * LANES:ATTN_WIDTH + (c + 1) * LANES])
            q_ref[rs, sl] = qc.astype(BF16)
            k_ref[rs, sl] = kc.astype(BF16)
            made += [qc, kc]
        v_ref[rs, :] = z_qkv[:, 2 * ATTN_WIDTH:].astype(BF16)
        return made

    norm = lambda r: _rmsnorm(x_ref[r * half:(r + 1) * half, :], g_ref[...]).astype(BF16)
    z_a = _dot(norm(0), w_ref[...])
    hn_b = norm(1)
    z_b_u = _dot(hn_b, w_ref[:, :q0])
    made = epilogue(0, z_a[:, :q0], z_a[:, q0:])
    parts = [m[r:r + SUBLANES, c:c + LANES] for m in made
             for r in range(0, half, SUBLANES) for c in range(0, m.shape[1], LANES)]
    while len(parts) > 1:
        pairs = [jnp.maximum(parts[n], parts[n + 1]) for n in range(0, len(parts) - 1, 2)]
        parts = pairs + parts[len(parts) - len(parts) % 2:]
    zero = parts[0] - parts[0]
    head = hn_b[:BF16_ROWS, :LANES] + jnp.concatenate([zero, zero], axis=0).astype(BF16)
    top = jnp.concatenate([head, hn_b[:BF16_ROWS, LANES:]], axis=1)
    hn_b = jnp.concatenate([top, hn_b[BF16_ROWS:]], axis=0)
    z_b_qkv = _dot(hn_b, w_ref[:, q0:])
    epilogue(1, z_b_u, z_b_qkv)


def _in_proj(x2, positions, inv_freq, g, w):
    t = x2.shape[0]
    rows = IN_ROWS
    pos3 = positions.reshape(t // rows, 1, rows)
    freq = jnp.broadcast_to(inv_freq[:, None], (ROPE_HALF, rows))
    expand = _rope_expand_matrix()
    row_spec = lambda width: pl.BlockSpec((rows, width), lambda i: (i, 0))
    full = lambda a: pl.BlockSpec(a.shape, lambda i: (0, 0))
    return pl.pallas_call(
        _in_proj_kernel,
        grid=(t // rows,),
        in_specs=[row_spec(D_MODEL), pl.BlockSpec((1, 1, rows), lambda i: (i, 0, 0)),
                  full(freq), full(expand), full(g),
                  pl.BlockSpec(w.shape, lambda i: (0, 0), pipeline_mode=pl.Buffered(1))],
        out_specs=[row_spec(CONV_CH), row_spec(ATTN_WIDTH), row_spec(ATTN_WIDTH), row_spec(ATTN_WIDTH)],
        out_shape=[jax.ShapeDtypeStruct((t, CONV_CH), F32),
                   jax.ShapeDtypeStruct((t, ATTN_WIDTH), BF16),
                   jax.ShapeDtypeStruct((t, ATTN_WIDTH), BF16),
                   jax.ShapeDtypeStruct((t, ATTN_WIDTH), BF16)],
        scratch_shapes=[pltpu.VMEM(w.shape, BF16)],
        compiler_params=pltpu.CompilerParams(
            dimension_semantics=("arbitrary",), vmem_limit_bytes=VMEM_LIMIT),
        name="in_proj",
    )(x2, pos3, freq, expand, g, w)


def _conv_rows(window, w_ref, b_ref, lg_ref, lb_ref, not_before):
    first = CONV_PAD - (CONV_KERNEL - 1)
    win_rows = CONV_ROWS + CONV_PAD
    a = not_before[:BF16_ROWS, :LANES].astype(F32)[0:1, :]
    hold = a - a
    accs = []
    for g in range(CONV_CH // LANES):
        lanes = slice(g * LANES, (g + 1) * LANES)
        win = window(lanes)
        acc = jnp.broadcast_to(b_ref[:, lanes] + hold, (CONV_ROWS, LANES))
        for phase in range(SUBLANES):
            shifted = win if phase == 0 else pltpu.roll(win, win_rows - phase, 0)
            for base in range(0, CONV_PAD + 1, SUBLANES):
                j = base + phase - first
                if 0 <= j < CONV_KERNEL:
                    acc = acc + w_ref[j:j + 1, lanes] * shifted[base:base + CONV_ROWS]
        accs.append(acc)
        hold = acc[0:1, :] - acc[0:1, :]
    acc = jnp.concatenate(accs, axis=1)
    mu = jnp.mean(acc, axis=-1, keepdims=True)
    dev = acc - mu
    var = jnp.mean(dev * dev, axis=-1, keepdims=True)
    y = dev * lax.rsqrt(var + EPS) * lg_ref[...] + lb_ref[...]
    return (y * jax.nn.sigmoid(y)).astype(BF16)


def _attn_kernel(n_weights, q_ref, k_ref, v_ref, *refs):
    w32_refs = refs[:n_weights]
    o_ref = refs[n_weights]
    w16_refs = refs[n_weights + 1:2 * n_weights + 1]
    ot_ref = refs[2 * n_weights + 1]
    for src, dst in zip(w32_refs, w16_refs):
        dst[...] = src[...].astype(BF16)

    seq = q_ref.shape[0]
    nb = seq // MOBA_BLOCK
    q = q_ref[...]
    k = k_ref[...]
    vt = v_ref[...].astype(F32).T.astype(BF16)
    lane = lax.broadcasted_iota(jnp.int32, (1, LANES), 1)
    kmean = jnp.mean(k.astype(F32).reshape(nb, MOBA_BLOCK, LANES), axis=1)
    heads = HEADS_PER_STEP
    cols = heads * MOBA_BLOCK
    key_row = lax.broadcasted_iota(jnp.int32, (MOBA_BLOCK, cols), 0)
    q_col = lax.broadcasted_iota(jnp.int32, (MOBA_BLOCK, cols), 1) & (MOBA_BLOCK - 1)
    causal = key_row <= q_col
    blk = lax.broadcasted_iota(jnp.int32, (nb, MOBA_BLOCK), 0)
    first_ranked = MOBA_TOPK + 1
    vt_aug = jnp.concatenate([vt, jnp.ones((BF16_ROWS, seq), BF16)], axis=0)

    in_head = [(lane >= h * HEAD_DIM) & (lane < (h + 1) * HEAD_DIM) for h in range(heads)]
    gate_t = [_dot_nt(jnp.where(m, kmean, 0.0).astype(BF16), q[first_ranked * MOBA_BLOCK:])
              for m in in_head]

    def scores(i):
        qi = q[i * MOBA_BLOCK:(i + 1) * MOBA_BLOCK]
        q2 = jnp.concatenate([jnp.where(m, qi, jnp.zeros_like(qi)) for m in in_head], axis=0)
        return _dot_nt(k[:(i + 1) * MOBA_BLOCK], q2)

    def finish(i, st):
        if i >= first_ranked:
            caps = []
            for h in range(heads):
                c0 = (i - first_ranked) * MOBA_BLOCK
                gi = gate_t[h][:, c0:c0 + MOBA_BLOCK]
                rank = jnp.zeros((nb, MOBA_BLOCK), jnp.int32)
                for m in range(i):
                    gm = gi[m:m + 1, :]
                    beats = (gm > gi) | ((gm == gi) & (blk > m))
                    rank = rank + jnp.where(beats, 1, 0)
                caps.append(jnp.where(rank < MOBA_TOPK, POS, NEG))
            cap = jnp.concatenate(caps, axis=1)
        pieces = []
        for j in range(i + 1):
            sj = st[j * MOBA_BLOCK:(j + 1) * MOBA_BLOCK]
            if j == i:
                sj = jnp.where(causal, sj, NEG)
            elif i >= first_ranked:
                sj = jnp.minimum(sj, cap[j:j + 1, :])
            pieces.append(sj)
        st = pieces[0] if i == 0 else jnp.concatenate(pieces, axis=0)
        m_col = jnp.max(st, axis=0, keepdims=True)
        p = jnp.exp2(st - m_col).astype(BF16)
        ot = _dot(vt_aug[:, :(i + 1) * MOBA_BLOCK], p)
        denom = ot[LANES:LANES + 1]
        for h in range(heads):
            rows = slice(h * HEAD_DIM, (h + 1) * HEAD_DIM)
            qcols = slice(h * MOBA_BLOCK, (h + 1) * MOBA_BLOCK)
            ot_ref[rows, i * MOBA_BLOCK:(i + 1) * MOBA_BLOCK] = ot[rows, qcols] / denom[:, qcols]

    st_next = scores(0)
    for i in range(nb):
        st = st_next
        if i + 1 < nb:
            st_next = scores(i + 1)
        finish(i, st)
    o_ref[...] = ot_ref[...].T.astype(BF16)


def _attn(q2, k2, v2, bsz, seq, weights):
    pairs = ATTN_WIDTH // LANES
    n_steps = bsz * pairs
    spec = pl.BlockSpec((seq, LANES), lambda b, hp: (b, hp))

    def slice_spec(w):
        n = next(n for n in (n_steps, n_steps // 2, n_steps // 4, 1)
                 if w.shape[0] % n == 0 and (w.shape[0] // n) % BF16_ROWS == 0)
        return pl.BlockSpec((w.shape[0] // n, w.shape[1]),
                            lambda b, hp: (jnp.minimum(b * pairs + hp, n - 1), 0))

    w_specs = [slice_spec(w) for w in weights]
    out = pl.pallas_call(
        functools.partial(_attn_kernel, len(weights)),
        grid=(bsz, pairs),
        in_specs=[spec, spec, spec] + w_specs,
        out_specs=[spec] + w_specs,
        out_shape=[jax.ShapeDtypeStruct(q2.shape, BF16)]
        + [jax.ShapeDtypeStruct(w.shape, BF16) for w in weights],
        scratch_shapes=[pltpu.VMEM((LANES, seq), F32)],
        compiler_params=pltpu.CompilerParams(
            dimension_semantics=("arbitrary", "arbitrary"), vmem_limit_bytes=VMEM_LIMIT),
        name="attn",
    )(q2, k2, v2, *weights)
    return out[0], out[1:]


def _ordered_after(x, anchor):
    rows, width = anchor.shape
    parts = [anchor[r:r + BF16_ROWS, c:c + LANES]
             for r in range(0, rows, BF16_ROWS) for c in range(0, width, LANES)]
    while len(parts) > 1:
        parts = [jnp.maximum(parts[n], parts[n + 1]) for n in range(0, len(parts), 2)]
    a = parts[0]
    head = x[:BF16_ROWS, :LANES] + (a - a)
    top = jnp.concatenate([head, x[:BF16_ROWS, LANES:]], axis=1)
    return jnp.concatenate([top, x[BF16_ROWS:]], axis=0)


def _tail_kernel(tiles_per_seq, u0_ref, u_ref, halo_ref, x_ref, a_ref, p_ref, cw_ref, cb_ref, lg_ref,
                 lb_ref, wout_ref, gf_ref, wup_ref, wdn_ref, gp_ref, wg_ref, wp_ref, gfin_ref,
                 o_ref, conv_ref):
    i = pl.program_id(0)
    n_tiles = pl.num_programs(0)

    def conv_chunk_of(tile_ref, halo, c, not_before):
        def window(lanes):
            start = c * CONV_ROWS - CONV_PAD
            stop = (c + 1) * CONV_ROWS
            if start < 0:
                return jnp.concatenate([halo[CONV_PAD + start:, lanes], tile_ref[0:stop, lanes]],
                                       axis=0)
            return tile_ref[start:stop, lanes]
        y = _conv_rows(window, cw_ref, cb_ref, lg_ref, lb_ref, not_before)
        conv_ref[c * CONV_ROWS:(c + 1) * CONV_ROWS, :] = y
        return y

    @pl.when(i == 0)
    def _():
        zeros = jnp.zeros((CONV_PAD, CONV_CH), F32)
        start = u0_ref[0:BF16_ROWS, 0:LANES].astype(BF16)
        for c in range(u0_ref.shape[0] // CONV_ROWS):
            conv_chunk_of(u0_ref, zeros, c, start)

    tile = jnp.minimum(i + 1, n_tiles - 1)
    starts_seq = lax.rem(tile, tiles_per_seq) == 0
    halo = jnp.where(starts_seq, 0.0, halo_ref[...])
    conv_chunk = functools.partial(conv_chunk_of, u_ref, halo)

    h = (x_ref[...]
         + _dot(conv_ref[...], wout_ref[:CONV_CH, :])
         + _dot(a_ref[...], wout_ref[CONV_CH:, :]))
    hn = _rmsnorm(h, gf_ref[...]).astype(BF16)
    n_chunks = u_ref.shape[0] // CONV_ROWS
    bounds = [0, (4 * n_chunks) // 8, (7 * n_chunks) // 8, n_chunks]

    def conv_group(q, x, not_before):
        for c in range(bounds[q], bounds[q + 1]):
            y = conv_chunk(c, not_before)
            x = _ordered_after(x, y)
            not_before = y
        return x

    gt = _dot(hn, wup_ref[:, :D_FF])
    hn = conv_group(0, hn, hn)
    up = _dot(hn, wup_ref[:, D_FF:])
    act = conv_group(1, (jax.nn.silu(gt) * up).astype(BF16), hn)
    h = h + _dot(act, wdn_ref[...])
    hn = conv_group(2, _rmsnorm(h, gp_ref[...]).astype(BF16), act)
    gate = jax.nn.sigmoid(_dot(hn, wg_ref[...]))
    h = h + gate * _dot(p_ref[...].astype(BF16), wp_ref[...])
    o_ref[...] = _rmsnorm(h, gfin_ref[...])


def _tail(u2, x2, attn2, p2, seq, cw, cb, lg, lb, wout, gf, wup, wdn, gp, wg, wp, gfin):
    t = x2.shape[0]
    rows = TAIL_ROWS
    n_tiles = t // rows
    halo_per_tile = rows // CONV_PAD
    nxt = lambda i: jnp.minimum(i + 1, n_tiles - 1)
    row_spec = lambda width: pl.BlockSpec((rows, width), lambda i: (i, 0))
    full = lambda a: pl.BlockSpec(a.shape, lambda i: (0, 0), pipeline_mode=pl.Buffered(1))
    return pl.pallas_call(
        functools.partial(_tail_kernel, seq // rows),
        grid=(n_tiles,),
        in_specs=[pl.BlockSpec((rows, CONV_CH), lambda i: (0, 0)),
                  pl.BlockSpec((rows, CONV_CH), lambda i: (nxt(i), 0)),
                  pl.BlockSpec((CONV_PAD, CONV_CH), lambda i: (nxt(i) * halo_per_tile - 1, 0)),
                  row_spec(D_MODEL), row_spec(ATTN_WIDTH), row_spec(PLE_DIM),
                  full(cw), full(cb), full(lg), full(lb),
                  full(wout), full(gf), full(wup), full(wdn), full(gp), full(wg), full(wp),
                  full(gfin)],
        out_specs=row_spec(D_MODEL),
        out_shape=jax.ShapeDtypeStruct((t, D_MODEL), F32),
        scratch_shapes=[pltpu.VMEM((rows, CONV_CH), BF16)],
        compiler_params=pltpu.CompilerParams(
            dimension_semantics=("arbitrary",), vmem_limit_bytes=VMEM_LIMIT),
        name="tail",
    )(u2, u2, u2, x2, attn2, p2, cw, cb, lg, lb, wout, gf, wup, wdn, gp, wg, wp, gfin)


def kernel(x, p, positions, norm_mix_g, w_in, conv_w, conv_b, conv_ln_g, conv_ln_b, w_out,
           norm_ffn_g, w_ffn_up, w_ffn_down, norm_ple_g, w_ple_gate, w_ple_proj, final_norm_g):
    bsz, seq, _ = x.shape
    t = bsz * seq
    depth = w_in.shape[0]
    row = lambda a: a.reshape(1, -1)
    inv_freq = ROPE_THETA ** (-jnp.arange(0, ROPE_DIM, 2, dtype=F32) / ROPE_DIM)
    h = x.reshape(t, D_MODEL)
    for i in range(depth):
        u, q, k, v = _in_proj(h, positions, inv_freq, row(norm_mix_g[i]), w_in[i])
        attn, (wout, wup, wdn, wg, wp) = _attn(
            q, k, v, bsz, seq,
            [w_out[i], w_ffn_up[i], w_ffn_down[i], w_ple_gate[i], w_ple_proj[i]])
        assert depth == 1
        h = _tail(u, h, attn, p[i].reshape(t, PLE_DIM), seq, conv_w[i], row(conv_b[i]),
                  row(conv_ln_g[i]), row(conv_ln_b[i]), wout,
                  row(norm_ffn_g[i]), wup, wdn, row(norm_ple_g[i]), wg, wp,
                  row(final_norm_g))
    return h.reshape(bsz, seq, D_MODEL)
```

```python
import functools

import jax
import jax.numpy as jnp
from jax import lax
from jax.experimental import pallas as pl
from jax.experimental.pallas import tpu as pltpu

D_MODEL = 1024
CONV_CH = 512
ATTN_HEADS = 8
HEAD_DIM = 64
ATTN_WIDTH = ATTN_HEADS * HEAD_DIM
IN_WIDTH = 2 * CONV_CH + 3 * ATTN_WIDTH
CONV_KERNEL = 31
MOBA_BLOCK = 256
MOBA_TOPK = 3
ROPE_THETA = 500000.0
ROPE_DIM = HEAD_DIM // 4
ROPE_HALF = ROPE_DIM // 2
D_FF = -(-8 * D_MODEL // (3 * 256)) * 256
PLE_DIM = 256
EPS = 1e-6

LANES = 128
SUBLANES = 8
BF16_ROWS = 16
LOG2E = 1.4426950408889634
HEADS_PER_STEP = LANES // HEAD_DIM
CONV_PAD = 32
CONV_ROWS = 32
IN_ROWS = 512
TAIL_ROWS = 512
VMEM_LIMIT = 56 * 1024 * 1024

F32 = jnp.float32
BF16 = jnp.bfloat16
NEG = float(jnp.finfo(jnp.float32).min)
POS = float(jnp.finfo(jnp.float32).max)


def _rmsnorm(x, g):
    return x * lax.rsqrt(jnp.mean(x * x, axis=-1, keepdims=True) + EPS) * g


def _dot(a, b):
    return jnp.dot(a, b, preferred_element_type=F32)


def _dot_nt(a, b):
    return lax.dot_general(a, b, (((1,), (1,)), ((), ())), preferred_element_type=F32)


def _rope_expand_matrix():
    r = jnp.arange(LANES)[:, None]
    l = jnp.arange(2 * LANES)[None, :]
    k = r % (2 * ROPE_HALF)
    piece_ok = r < 3 * 2 * ROPE_HALF
    d = (l % LANES) % HEAD_DIM
    want = d % ROPE_HALF + ROPE_HALF * (l // LANES)
    return (piece_ok & (d < ROPE_DIM) & (k == want)).astype(BF16)


def _in_proj_kernel(x_ref, pos_ref, freq_ref, expand_ref, g_ref, w32_ref,
                    u_ref, q_ref, k_ref, v_ref, w_ref):
    @pl.when(pl.program_id(0) == 0)
    def _():
        w_ref[...] = w32_ref[...].astype(BF16)

    rows = x_ref.shape[0]
    half = rows // 2

    ang = pos_ref[0].astype(F32) * freq_ref[...]
    small = jnp.concatenate([jnp.cos(ang), jnp.sin(ang)], axis=0)
    hi = small.astype(BF16).astype(F32)
    mid = (small - hi).astype(BF16).astype(F32)
    lo = (small - hi - mid).astype(BF16).astype(F32)
    pad = jnp.zeros((LANES - 3 * small.shape[0], small.shape[1]), F32)
    pieces = jnp.concatenate([hi, mid, lo, pad], axis=0).T.astype(BF16)
    table = _dot(pieces, expand_ref[...])
    d = lax.broadcasted_iota(jnp.int32, (1, LANES), 1) & (HEAD_DIM - 1)
    cos = table[:, :LANES] + jnp.where(d < ROPE_DIM, 0.0, 1.0)
    sin = table[:, LANES:]
    sin_from_hi = jnp.where(d < ROPE_HALF, -sin, 0.0)
    sin_from_lo = jnp.where(d >= ROPE_HALF, sin, 0.0)

    q0 = 2 * CONV_CH
    k0 = q0 + ATTN_WIDTH
    v0 = k0 + ATTN_WIDTH
    scale = HEAD_DIM ** -0.5 * LOG2E

    def epilogue(r, z_u, z_qkv):
        rs = slice(r * half, (r + 1) * half)
        u = z_u[:, :CONV_CH] * jax.nn.sigmoid(z_u[:, CONV_CH:])
        u_ref[rs, :] = u
        made = [u]

        def rope(t):
            return (t * cos[rs]
                    + pltpu.roll(t, ROPE_HALF, 1) * sin_from_lo[rs]
                    + pltpu.roll(t, LANES - ROPE_HALF, 1) * sin_from_hi[rs])

        for c in range(ATTN_WIDTH // LANES):
            sl = slice(c * LANES, (c + 1) * LANES)
            qc = rope(z_qkv[:, c * LANES:(c + 1) * LANES]) * scale
            kc = rope(z_qkv[:, ATTN_WIDTH + c * LANES:ATTN_WIDTH + (c + 1) * LANES])
            q_ref[rs, sl] = qc.astype(BF16)
            k_ref[rs, sl] = kc.astype(BF16)
            made += [qc, kc]
        v_ref[rs, :] = z_qkv[:, 2 * ATTN_WIDTH:].astype(BF16)
        return made

    norm = lambda r: _rmsnorm(x_ref[r * half:(r + 1) * half, :], g_ref[...]).astype(BF16)
    z_a = _dot(norm(0), w_ref[...])
    hn_b = norm(1)
    z_b_u = _dot(hn_b, w_ref[:, :q0])
    made = epilogue(0, z_a[:, :q0], z_a[:, q0:])
    parts = [m[r:r + SUBLANES, c:c + LANES] for m in made
             for r in range(0, half, SUBLANES) for c in range(0, m.shape[1], LANES)]
    while len(parts) > 1:
        pairs = [jnp.maximum(parts[n], parts[n + 1]) for n in range(0, len(parts) - 1, 2)]
        parts = pairs + parts[len(parts) - len(parts) % 2:]
    zero = parts[0] - parts[0]
    head = hn_b[:BF16_ROWS, :LANES] + jnp.concatenate([zero, zero], axis=0).astype(BF16)
    top = jnp.concatenate([head, hn_b[:BF16_ROWS, LANES:]], axis=1)
    hn_b = jnp.concatenate([top, hn_b[BF16_ROWS:]], axis=0)
    z_b_qkv = _dot(hn_b, w_ref[:, q0:])
    epilogue(1, z_b_u, z_b_qkv)


def _in_proj(x2, positions, inv_freq, g, w):
    t = x2.shape[0]
    rows = IN_ROWS
    pos3 = positions.reshape(t // rows, 1, rows)
    freq = jnp.broadcast_to(inv_freq[:, None], (ROPE_HALF, rows))
    expand = _rope_expand_matrix()
    row_spec = lambda width: pl.BlockSpec((rows, width), lambda i: (i, 0))
    full = lambda a: pl.BlockSpec(a.shape, lambda i: (0, 0))
    return pl.pallas_call(
        _in_proj_kernel,
        grid=(t // rows,),
        in_specs=[row_spec(D_MODEL), pl.BlockSpec((1, 1, rows), lambda i: (i, 0, 0)),
                  full(freq), full(expand), full(g),
                  pl.BlockSpec(w.shape, lambda i: (0, 0), pipeline_mode=pl.Buffered(1))],
        out_specs=[row_spec(CONV_CH), row_spec(ATTN_WIDTH), row_spec(ATTN_WIDTH), row_spec(ATTN_WIDTH)],
        out_shape=[jax.ShapeDtypeStruct((t, CONV_CH), F32),
                   jax.ShapeDtypeStruct((t, ATTN_WIDTH), BF16),
                   jax.ShapeDtypeStruct((t, ATTN_WIDTH), BF16),
                   jax.ShapeDtypeStruct((t, ATTN_WIDTH), BF16)],
        scratch_shapes=[pltpu.VMEM(w.shape, BF16)],
        compiler_params=pltpu.CompilerParams(
            dimension_semantics=("arbitrary",), vmem_limit_bytes=VMEM_LIMIT),
        name="in_proj",
    )(x2, pos3, freq, expand, g, w)


def _conv_rows(window, w_ref, b_ref, lg_ref, lb_ref, not_before):
    first = CONV_PAD - (CONV_KERNEL - 1)
    win_rows = CONV_ROWS + CONV_PAD
    a = not_before[:BF16_ROWS, :LANES].astype(F32)[0:1, :]
    hold = a - a
    accs = []
    for g in range(CONV_CH // LANES):
        lanes = slice(g * LANES, (g + 1) * LANES)
        win = window(lanes)
        acc = jnp.broadcast_to(b_ref[:, lanes] + hold, (CONV_ROWS, LANES))
        for phase in range(SUBLANES):
            shifted = win if phase == 0 else pltpu.roll(win, win_rows - phase, 0)
            for base in range(0, CONV_PAD + 1, SUBLANES):
                j = base + phase - first
                if 0 <= j < CONV_KERNEL:
                    acc = acc + w_ref[j:j + 1, lanes] * shifted[base:base + CONV_ROWS]
        accs.append(acc)
        hold = acc[0:1, :] - acc[0:1, :]
    acc = jnp.concatenate(accs, axis=1)
    mu = jnp.mean(acc, axis=-1, keepdims=True)
    dev = acc - mu
    var = jnp.mean(dev * dev, axis=-1, keepdims=True)
    y = dev * lax.rsqrt(var + EPS) * lg_ref[...] + lb_ref[...]
    return (y * jax.nn.sigmoid(y)).astype(BF16)


def _attn_kernel(n_weights, q_ref, k_ref, v_ref, *refs):
    w32_refs = refs[:n_weights]
    o_ref = refs[n_weights]
    w16_refs = refs[n_weights + 1:]
    for src, dst in zip(w32_refs, w16_refs):
        dst[...] = src[...].astype(BF16)

    seq = q_ref.shape[0]
    nb = seq // MOBA_BLOCK
    q = q_ref[...]
    k = k_ref[...]
    vt = v_ref[...].astype(F32).T.astype(BF16)
    lane = lax.broadcasted_iota(jnp.int32, (1, LANES), 1)
    kmean = jnp.mean(k.astype(F32).reshape(nb, MOBA_BLOCK, LANES), axis=1)
    heads = HEADS_PER_STEP
    cols = heads * MOBA_BLOCK
    key_row = lax.broadcasted_iota(jnp.int32, (MOBA_BLOCK, cols), 0)
    q_col = lax.broadcasted_iota(jnp.int32, (MOBA_BLOCK, cols), 1) & (MOBA_BLOCK - 1)
    causal = key_row <= q_col
    blk = lax.broadcasted_iota(jnp.int32, (nb, MOBA_BLOCK), 0)
    first_ranked = MOBA_TOPK + 1
    vt_aug = jnp.concatenate([vt, jnp.ones((BF16_ROWS, seq), BF16)], axis=0)

    in_head = [(lane >= h * HEAD_DIM) & (lane < (h + 1) * HEAD_DIM) for h in range(heads)]
    gate_t = [_dot_nt(jnp.where(m, kmean, 0.0).astype(BF16), q[first_ranked * MOBA_BLOCK:])
              for m in in_head]

    def scores(i):
        qi = q[i * MOBA_BLOCK:(i + 1) * MOBA_BLOCK]
        q2 = jnp.concatenate([jnp.where(m, qi, jnp.zeros_like(qi)) for m in in_head], axis=0)
        return _dot_nt(k[:(i + 1) * MOBA_BLOCK], q2)

    def finish(i, st):
        if i >= first_ranked:
            caps = []
            for h in range(heads):
                c0 = (i - first_ranked) * MOBA_BLOCK
                gi = gate_t[h][:, c0:c0 + MOBA_BLOCK]
                rank = jnp.zeros((nb, MOBA_BLOCK), jnp.int32)
                for m in range(i):
                    gm = gi[m:m + 1, :]
                    beats = (gm > gi) | ((gm == gi) & (blk > m))
                    rank = rank + jnp.where(beats, 1, 0)
                caps.append(jnp.where(rank < MOBA_TOPK, POS, NEG))
            cap = jnp.concatenate(caps, axis=1)
        pieces = []
        for j in range(i + 1):
            sj = st[j * MOBA_BLOCK:(j + 1) * MOBA_BLOCK]
            if j == i:
                sj = jnp.where(causal, sj, NEG)
            elif i >= first_ranked:
                sj = jnp.minimum(sj, cap[j:j + 1, :])
            pieces.append(sj)
        st = pieces[0] if i == 0 else jnp.concatenate(pieces, axis=0)
        m_col = jnp.max(st, axis=0, keepdims=True)
        p = jnp.exp2(st - m_col).astype(BF16)
        ot = _dot(vt_aug[:, :(i + 1) * MOBA_BLOCK], p)
        denom = ot[LANES:LANES + 1]
        out_t = jnp.concatenate(
            [ot[h * HEAD_DIM:(h + 1) * HEAD_DIM, h * MOBA_BLOCK:(h + 1) * MOBA_BLOCK]
             / denom[:, h * MOBA_BLOCK:(h + 1) * MOBA_BLOCK] for h in range(heads)], axis=0)
        o_ref[i * MOBA_BLOCK:(i + 1) * MOBA_BLOCK, :] = out_t.T.astype(BF16)

    st_next = scores(0)
    for i in range(nb):
        st = st_next
        if i + 1 < nb:
            st_next = scores(i + 1)
        finish(i, st)


def _attn(q2, k2, v2, bsz, seq, weights):
    pairs = ATTN_WIDTH // LANES
    n_steps = bsz * pairs
    spec = pl.BlockSpec((seq, LANES), lambda b, hp: (b, hp))

    def slice_spec(w):
        n = next(n for n in (n_steps, n_steps // 2, n_steps // 4, 1)
                 if w.shape[0] % n == 0 and (w.shape[0] // n) % BF16_ROWS == 0)
        return pl.BlockSpec((w.shape[0] // n, w.shape[1]),
                            lambda b, hp: (jnp.minimum(b * pairs + hp, n - 1), 0))

    w_specs = [slice_spec(w) for w in weights]
    out = pl.pallas_call(
        functools.partial(_attn_kernel, len(weights)),
        grid=(bsz, pairs),
        in_specs=[spec, spec, spec] + w_specs,
        out_specs=[spec] + w_specs,
        out_shape=[jax.ShapeDtypeStruct(q2.shape, BF16)]
        + [jax.ShapeDtypeStruct(w.shape, BF16) for w in weights],
        compiler_params=pltpu.CompilerParams(
            dimension_semantics=("arbitrary", "arbitrary"), vmem_limit_bytes=VMEM_LIMIT),
        name="attn",
    )(q2, k2, v2, *weights)
    return out[0], out[1:]


def _ordered_after(x, anchor):
    rows, width = anchor.shape
    parts = [anchor[r:r + BF16_ROWS, c:c + LANES]
             for r in range(0, rows, BF16_ROWS) for c in range(0, width, LANES)]
    while len(parts) > 1:
        parts = [jnp.maximum(parts[n], parts[n + 1]) for n in range(0, len(parts), 2)]
    a = parts[0]
    head = x[:BF16_ROWS, :LANES] + (a - a)
    top = jnp.concatenate([head, x[:BF16_ROWS, LANES:]], axis=1)
    return jnp.concatenate([top, x[BF16_ROWS:]], axis=0)


def _tail_kernel(tiles_per_seq, u0_ref, u_ref, halo_ref, x_ref, a_ref, p_ref, cw_ref, cb_ref, lg_ref,
                 lb_ref, wout_ref, gf_ref, wup_ref, wdn_ref, gp_ref, wg_ref, wp_ref, gfin_ref,
                 o_ref, conv_ref):
    i = pl.program_id(0)
    n_tiles = pl.num_programs(0)

    def conv_chunk_of(tile_ref, halo, c, not_before):
        def window(lanes):
            start = c * CONV_ROWS - CONV_PAD
            stop = (c + 1) * CONV_ROWS
            if start < 0:
                return jnp.concatenate([halo[CONV_PAD + start:, lanes], tile_ref[0:stop, lanes]],
                                       axis=0)
            return tile_ref[start:stop, lanes]
        y = _conv_rows(window, cw_ref, cb_ref, lg_ref, lb_ref, not_before)
        conv_ref[c * CONV_ROWS:(c + 1) * CONV_ROWS, :] = y
        return y

    @pl.when(i == 0)
    def _():
        zeros = jnp.zeros((CONV_PAD, CONV_CH), F32)
        start = u0_ref[0:BF16_ROWS, 0:LANES].astype(BF16)
        for c in range(u0_ref.shape[0] // CONV_ROWS):
            conv_chunk_of(u0_ref, zeros, c, start)

    tile = jnp.minimum(i + 1, n_tiles - 1)
    starts_seq = lax.rem(tile, tiles_per_seq) == 0
    halo = jnp.where(starts_seq, 0.0, halo_ref[...])
    conv_chunk = functools.partial(conv_chunk_of, u_ref, halo)

    h = (x_ref[...]
         + _dot(conv_ref[...], wout_ref[:CONV_CH, :])
         + _dot(a_ref[...], wout_ref[CONV_CH:, :]))
    hn = _rmsnorm(h, gf_ref[...]).astype(BF16)
    n_chunks = u_ref.shape[0] // CONV_ROWS
    bounds = [0, (4 * n_chunks) // 8, (7 * n_chunks) // 8, n_chunks]

    def conv_group(q, x, not_before):
        for c in range(bounds[q], bounds[q + 1]):
            y = conv_chunk(c, not_before)
            x = _ordered_after(x, y)
            not_before = y
        return x

    gt = _dot(hn, wup_ref[:, :D_FF])
    hn = conv_group(0, hn, hn)
    up = _dot(hn, wup_ref[:, D_FF:])
    act = conv_group(1, (jax.nn.silu(gt) * up).astype(BF16), hn)
    h = h + _dot(act, wdn_ref[...])
    hn = conv_group(2, _rmsnorm(h, gp_ref[...]).astype(BF16), act)
    gate = jax.nn.sigmoid(_dot(hn, wg_ref[...]))
    h = h + gate * _dot(p_ref[...].astype(BF16), wp_ref[...])
    o_ref[...] = _rmsnorm(h, gfin_ref[...])


def _tail(u2, x2, attn2, p2, seq, cw, cb, lg, lb, wout, gf, wup, wdn, gp, wg, wp, gfin):
    t = x2.shape[0]
    rows = TAIL_ROWS
    n_tiles = t // rows
    halo_per_tile = rows // CONV_PAD
    nxt = lambda i: jnp.minimum(i + 1, n_tiles - 1)
    row_spec = lambda width: pl.BlockSpec((rows, width), lambda i: (i, 0))
    full = lambda a: pl.BlockSpec(a.shape, lambda i: (0, 0), pipeline_mode=pl.Buffered(1))
    return pl.pallas_call(
        functools.partial(_tail_kernel, seq // rows),
        grid=(n_tiles,),
        in_specs=[pl.BlockSpec((rows, CONV_CH), lambda i: (0, 0)),
                  pl.BlockSpec((rows, CONV_CH), lambda i: (nxt(i), 0)),
                  pl.BlockSpec((CONV_PAD, CONV_CH), lambda i: (nxt(i) * halo_per_tile - 1, 0)),
                  row_spec(D_MODEL), row_spec(ATTN_WIDTH), row_spec(PLE_DIM),
                  full(cw), full(cb), full(lg), full(lb),
                  full(wout), full(gf), full(wup), full(wdn), full(gp), full(wg), full(wp),
                  full(gfin)],
        out_specs=row_spec(D_MODEL),
        out_shape=jax.ShapeDtypeStruct((t, D_MODEL), F32),
        scratch_shapes=[pltpu.VMEM((rows, CONV_CH), BF16)],
        compiler_params=pltpu.CompilerParams(
            dimension_semantics=("arbitrary",), vmem_limit_bytes=VMEM_LIMIT),
        name="tail",
    )(u2, u2, u2, x2, attn2, p2, cw, cb, lg, lb, wout, gf, wup, wdn, gp, wg, wp, gfin)


def kernel(x, p, positions, norm_mix_g, w_in, conv_w, conv_b, conv_ln_g, conv_ln_b, w_out,
           norm_ffn_g, w_ffn_up, w_ffn_down, norm_ple_g, w_ple_gate, w_ple_proj, final_norm_g):
    bsz, seq, _ = x.shape
    t = bsz * seq
    depth = w_in.shape[0]
    row = lambda a: a.reshape(1, -1)
    inv_freq = ROPE_THETA ** (-jnp.arange(0, ROPE_DIM, 2, dtype=F32) / ROPE_DIM)
    h = x.reshape(t, D_MODEL)
    for i in range(depth):
        u, q, k, v = _in_proj(h, positions, inv_freq, row(norm_mix_g[i]), w_in[i])
        attn, (wout, wup, wdn, wg, wp) = _attn(
            q, k, v, bsz, seq,
            [w_out[i], w_ffn_up[i], w_ffn_down[i], w_ple_gate[i], w_ple_proj[i]])
        assert depth == 1
        h = _tail(u, h, attn, p[i].reshape(t, PLE_DIM), seq, conv_w[i], row(conv_b[i]),
                  row(conv_ln_g[i]), row(conv_ln_b[i]), wout,
                  row(norm_ffn_g[i]), wup, wdn, row(norm_ple_g[i]), wg, wp,
                  row(final_norm_g))
    return h.reshape(bsz, seq, D_MODEL)
```

```python
import functools

import jax
import jax.numpy as jnp
from jax import lax
from jax.experimental import pallas as pl
from jax.experimental.pallas import tpu as pltpu

D_MODEL = 1024
CONV_CH = 512
ATTN_HEADS = 8
HEAD_DIM = 64
ATTN_WIDTH = ATTN_HEADS * HEAD_DIM
IN_WIDTH = 2 * CONV_CH + 3 * ATTN_WIDTH
CONV_KERNEL = 31
MOBA_BLOCK = 256
MOBA_TOPK = 3
ROPE_THETA = 500000.0
ROPE_DIM = HEAD_DIM // 4
ROPE_HALF = ROPE_DIM // 2
D_FF = -(-8 * D_MODEL // (3 * 256)) * 256
PLE_DIM = 256
EPS = 1e-6

LANES = 128
SUBLANES = 8
BF16_ROWS = 16
LOG2E = 1.4426950408889634
HEADS_PER_STEP = LANES // HEAD_DIM
CONV_PAD = 32
CONV_ROWS = 32
IN_ROWS = 1024
IN_PIECE = 256
TAIL_ROWS = 512
VMEM_LIMIT = 56 * 1024 * 1024

F32 = jnp.float32
BF16 = jnp.bfloat16
NEG = float(jnp.finfo(jnp.float32).min)
POS = float(jnp.finfo(jnp.float32).max)


def _rmsnorm(x, g):
    return x * lax.rsqrt(jnp.mean(x * x, axis=-1, keepdims=True) + EPS) * g


def _dot(a, b):
    return jnp.dot(a, b, preferred_element_type=F32)


def _dot_nt(a, b):
    return lax.dot_general(a, b, (((1,), (1,)), ((), ())), preferred_element_type=F32)


def _rope_expand_matrix():
    r = jnp.arange(LANES)[:, None]
    l = jnp.arange(2 * LANES)[None, :]
    k = r % (2 * ROPE_HALF)
    piece_ok = r < 3 * 2 * ROPE_HALF
    d = (l % LANES) % HEAD_DIM
    want = d % ROPE_HALF + ROPE_HALF * (l // LANES)
    return (piece_ok & (d < ROPE_DIM) & (k == want)).astype(BF16)


def _in_proj_kernel(x_ref, pos_ref, freq_ref, expand_ref, g_ref, w32_ref,
                    u_ref, q_ref, k_ref, v_ref, w_ref):
    @pl.when(pl.program_id(0) == 0)
    def _():
        w_ref[...] = w32_ref[...].astype(BF16)

    rows = x_ref.shape[0]
    piece = IN_PIECE
    assert rows % piece == 0

    ang = pos_ref[0].astype(F32) * freq_ref[...]
    small = jnp.concatenate([jnp.cos(ang), jnp.sin(ang)], axis=0)
    hi = small.astype(BF16).astype(F32)
    mid = (small - hi).astype(BF16).astype(F32)
    lo = (small - hi - mid).astype(BF16).astype(F32)
    pad = jnp.zeros((LANES - 3 * small.shape[0], small.shape[1]), F32)
    pieces = jnp.concatenate([hi, mid, lo, pad], axis=0).T.astype(BF16)
    table = _dot(pieces, expand_ref[...])
    d = lax.broadcasted_iota(jnp.int32, (1, LANES), 1) & (HEAD_DIM - 1)
    cos = table[:, :LANES] + jnp.where(d < ROPE_DIM, 0.0, 1.0)
    sin = table[:, LANES:]
    sin_from_hi = jnp.where(d < ROPE_HALF, -sin, 0.0)
    sin_from_lo = jnp.where(d >= ROPE_HALF, sin, 0.0)

    q0 = 2 * CONV_CH
    k0 = q0 + ATTN_WIDTH
    v0 = k0 + ATTN_WIDTH
    scale = HEAD_DIM ** -0.5 * LOG2E

    def epilogue(r, z_u, z_qkv):
        rs = slice(r * piece, (r + 1) * piece)
        u = z_u[:, :CONV_CH] * jax.nn.sigmoid(z_u[:, CONV_CH:])
        u_ref[rs, :] = u
        made = [u]

        def rope(t):
            return (t * cos[rs]
                    + pltpu.roll(t, ROPE_HALF, 1) * sin_from_lo[rs]
                    + pltpu.roll(t, LANES - ROPE_HALF, 1) * sin_from_hi[rs])

        for c in range(ATTN_WIDTH // LANES):
            sl = slice(c * LANES, (c + 1) * LANES)
            qc = rope(z_qkv[:, c * LANES:(c + 1) * LANES]) * scale
            kc = rope(z_qkv[:, ATTN_WIDTH + c * LANES:ATTN_WIDTH + (c + 1) * LANES])
            q_ref[rs, sl] = qc.astype(BF16)
            k_ref[rs, sl] = kc.astype(BF16)
            made += [qc, kc]
        v_ref[rs, :] = z_qkv[:, 2 * ATTN_WIDTH:].astype(BF16)
        return made

    norm = lambda r: _rmsnorm(x_ref[r * piece:(r + 1) * piece, :], g_ref[...]).astype(BF16)
    prev = None
    for r in range(rows // piece):
        hn = norm(r)
        z_u = _dot(hn, w_ref[:, :q0])
        if prev is not None:
            made = epilogue(r - 1, *prev)
            parts = [m[a:a + SUBLANES, c:c + LANES] for m in made
                     for a in range(0, piece, SUBLANES) for c in range(0, m.shape[1], LANES)]
            while len(parts) > 1:
                pairs = [jnp.maximum(parts[n], parts[n + 1]) for n in range(0, len(parts) - 1, 2)]
                parts = pairs + parts[len(parts) - len(parts) % 2:]
            zero = parts[0] - parts[0]
            head = hn[:BF16_ROWS, :LANES] + jnp.concatenate([zero, zero], axis=0).astype(BF16)
            top = jnp.concatenate([head, hn[:BF16_ROWS, LANES:]], axis=1)
            hn = jnp.concatenate([top, hn[BF16_ROWS:]], axis=0)
        prev = (z_u, _dot(hn, w_ref[:, q0:]))
    epilogue(rows // piece - 1, *prev)


def _in_proj(x2, positions, inv_freq, g, w):
    t = x2.shape[0]
    rows = IN_ROWS
    pos3 = positions.reshape(t // rows, 1, rows)
    freq = jnp.broadcast_to(inv_freq[:, None], (ROPE_HALF, rows))
    expand = _rope_expand_matrix()
    row_spec = lambda width: pl.BlockSpec((rows, width), lambda i: (i, 0))
    full = lambda a: pl.BlockSpec(a.shape, lambda i: (0, 0))
    return pl.pallas_call(
        _in_proj_kernel,
        grid=(t // rows,),
        in_specs=[row_spec(D_MODEL), pl.BlockSpec((1, 1, rows), lambda i: (i, 0, 0)),
                  full(freq), full(expand), full(g),
                  pl.BlockSpec(w.shape, lambda i: (0, 0), pipeline_mode=pl.Buffered(1))],
        out_specs=[row_spec(CONV_CH), row_spec(ATTN_WIDTH), row_spec(ATTN_WIDTH), row_spec(ATTN_WIDTH)],
        out_shape=[jax.ShapeDtypeStruct((t, CONV_CH), F32),
                   jax.ShapeDtypeStruct((t, ATTN_WIDTH), BF16),
                   jax.ShapeDtypeStruct((t, ATTN_WIDTH), BF16),
                   jax.ShapeDtypeStruct((t, ATTN_WIDTH), BF16)],
        scratch_shapes=[pltpu.VMEM(w.shape, BF16)],
        compiler_params=pltpu.CompilerParams(
            dimension_semantics=("arbitrary",), vmem_limit_bytes=VMEM_LIMIT),
        name="in_proj",
    )(x2, pos3, freq, expand, g, w)


def _conv_rows(window, w_ref, b_ref, lg_ref, lb_ref, not_before):
    first = CONV_PAD - (CONV_KERNEL - 1)
    win_rows = CONV_ROWS + CONV_PAD
    a = not_before[:BF16_ROWS, :LANES].astype(F32)[0:1, :]
    hold = a - a
    accs = []
    for g in range(CONV_CH // LANES):
        lanes = slice(g * LANES, (g + 1) * LANES)
        win = window(lanes)
        acc = jnp.broadcast_to(b_ref[:, lanes] + hold, (CONV_ROWS, LANES))
        for phase in range(SUBLANES):
            shifted = win if phase == 0 else pltpu.roll(win, win_rows - phase, 0)
            for base in range(0, CONV_PAD + 1, SUBLANES):
                j = base + phase - first
                if 0 <= j < CONV_KERNEL:
                    acc = acc + w_ref[j:j + 1, lanes] * shifted[base:base + CONV_ROWS]
        accs.append(acc)
        hold = acc[0:1, :] - acc[0:1, :]
    acc = jnp.concatenate(accs, axis=1)
    mu = jnp.mean(acc, axis=-1, keepdims=True)
    dev = acc - mu
    var = jnp.mean(dev * dev, axis=-1, keepdims=True)
    y = dev * lax.rsqrt(var + EPS) * lg_ref[...] + lb_ref[...]
    return (y * jax.nn.sigmoid(y)).astype(BF16)


def _attn_kernel(n_weights, q_ref, k_ref, v_ref, *refs):
    w32_refs = refs[:n_weights]
    o_ref = refs[n_weights]
    w16_refs = refs[n_weights + 1:]
    for src, dst in zip(w32_refs, w16_refs):
        dst[...] = src[...].astype(BF16)

    seq = q_ref.shape[0]
    nb = seq // MOBA_BLOCK
    q = q_ref[...]
    k = k_ref[...]
    vt = v_ref[...].astype(F32).T.astype(BF16)
    lane = lax.broadcasted_iota(jnp.int32, (1, LANES), 1)
    kmean = jnp.mean(k.astype(F32).reshape(nb, MOBA_BLOCK, LANES), axis=1)
    heads = HEADS_PER_STEP
    cols = heads * MOBA_BLOCK
    key_row = lax.broadcasted_iota(jnp.int32, (MOBA_BLOCK, cols), 0)
    q_col = lax.broadcasted_iota(jnp.int32, (MOBA_BLOCK, cols), 1) & (MOBA_BLOCK - 1)
    causal = key_row <= q_col
    blk = lax.broadcasted_iota(jnp.int32, (nb, MOBA_BLOCK), 0)
    first_ranked = MOBA_TOPK + 1
    vt_aug = jnp.concatenate([vt, jnp.ones((BF16_ROWS, seq), BF16)], axis=0)

    in_head = [(lane >= h * HEAD_DIM) & (lane < (h + 1) * HEAD_DIM) for h in range(heads)]
    gate_t = [_dot_nt(jnp.where(m, kmean, 0.0).astype(BF16), q[first_ranked * MOBA_BLOCK:])
              for m in in_head]

    def scores(i):
        qi = q[i * MOBA_BLOCK:(i + 1) * MOBA_BLOCK]
        q2 = jnp.concatenate([jnp.where(m, qi, jnp.zeros_like(qi)) for m in in_head], axis=0)
        return _dot_nt(k[:(i + 1) * MOBA_BLOCK], q2)

    def finish(i, st):
        if i >= first_ranked:
            caps = []
            for h in range(heads):
                c0 = (i - first_ranked) * MOBA_BLOCK
                gi = gate_t[h][:, c0:c0 + MOBA_BLOCK]
                rank = jnp.zeros((nb, MOBA_BLOCK), jnp.int32)
                for m in range(i):
                    gm = gi[m:m + 1, :]
                    beats = (gm > gi) | ((gm == gi) & (blk > m))
                    rank = rank + jnp.where(beats, 1, 0)
                caps.append(jnp.where(rank < MOBA_TOPK, POS, NEG))
            cap = jnp.concatenate(caps, axis=1)
        pieces = []
        for j in range(i + 1):
            sj = st[j * MOBA_BLOCK:(j + 1) * MOBA_BLOCK]
            if j == i:
                sj = jnp.where(causal, sj, NEG)
            elif i >= first_ranked:
                sj = jnp.minimum(sj, cap[j:j + 1, :])
            pieces.append(sj)
        st = pieces[0] if i == 0 else jnp.concatenate(pieces, axis=0)
        m_col = jnp.max(st, axis=0, keepdims=True)
        p = jnp.exp2(st - m_col).astype(BF16)
        ot = _dot(vt_aug[:, :(i + 1) * MOBA_BLOCK], p)
        denom = ot[LANES:LANES + 1]
        out_t = jnp.concatenate(
            [ot[h * HEAD_DIM:(h + 1) * HEAD_DIM, h * MOBA_BLOCK:(h + 1) * MOBA_BLOCK]
             / denom[:, h * MOBA_BLOCK:(h + 1) * MOBA_BLOCK] for h in range(heads)], axis=0)
        o_ref[i * MOBA_BLOCK:(i + 1) * MOBA_BLOCK, :] = out_t.T.astype(BF16)

    st_next = scores(0)
    for i in range(nb):
        st = st_next
        if i + 1 < nb:
            st_next = scores(i + 1)
        finish(i, st)


def _attn(q2, k2, v2, bsz, seq, weights):
    pairs = ATTN_WIDTH // LANES
    n_steps = bsz * pairs
    spec = pl.BlockSpec((seq, LANES), lambda b, hp: (b, hp))

    def slice_spec(w):
        n = next(n for n in (n_steps, n_steps // 2, n_steps // 4, 1)
                 if w.shape[0] % n == 0 and (w.shape[0] // n) % BF16_ROWS == 0)
        return pl.BlockSpec((w.shape[0] // n, w.shape[1]),
                            lambda b, hp: (jnp.minimum(b * pairs + hp, n - 1), 0))

    w_specs = [slice_spec(w) for w in weights]
    out = pl.pallas_call(
        functools.partial(_attn_kernel, len(weights)),
        grid=(bsz, pairs),
        in_specs=[spec, spec, spec] + w_specs,
        out_specs=[spec] + w_specs,
        out_shape=[jax.ShapeDtypeStruct(q2.shape, BF16)]
        + [jax.ShapeDtypeStruct(w.shape, BF16) for w in weights],
        compiler_params=pltpu.CompilerParams(
            dimension_semantics=("arbitrary", "arbitrary"), vmem_limit_bytes=VMEM_LIMIT),
        name="attn",
    )(q2, k2, v2, *weights)
    return out[0], out[1:]


def _ordered_after(x, anchor):
    rows, width = anchor.shape
    parts = [anchor[r:r + BF16_ROWS, c:c + LANES]
             for r in range(0, rows, BF16_ROWS) for c in range(0, width, LANES)]
    while len(parts) > 1:
        parts = [jnp.maximum(parts[n], parts[n + 1]) for n in range(0, len(parts), 2)]
    a = parts[0]
    head = x[:BF16_ROWS, :LANES] + (a - a)
    top = jnp.concatenate([head, x[:BF16_ROWS, LANES:]], axis=1)
    return jnp.concatenate([top, x[BF16_ROWS:]], axis=0)


def _tail_kernel(tiles_per_seq, u0_ref, u_ref, halo_ref, x_ref, a_ref, p_ref, cw_ref, cb_ref, lg_ref,
                 lb_ref, wout_ref, gf_ref, wup_ref, wdn_ref, gp_ref, wg_ref, wp_ref, gfin_ref,
                 o_ref, conv_ref):
    i = pl.program_id(0)
    n_tiles = pl.num_programs(0)

    def conv_chunk_of(tile_ref, halo, c, not_before):
        def window(lanes):
            start = c * CONV_ROWS - CONV_PAD
            stop = (c + 1) * CONV_ROWS
            if start < 0:
                return jnp.concatenate([halo[CONV_PAD + start:, lanes], tile_ref[0:stop, lanes]],
                                       axis=0)
            return tile_ref[start:stop, lanes]
        y = _conv_rows(window, cw_ref, cb_ref, lg_ref, lb_ref, not_before)
        conv_ref[c * CONV_ROWS:(c + 1) * CONV_ROWS, :] = y
        return y

    @pl.when(i == 0)
    def _():
        zeros = jnp.zeros((CONV_PAD, CONV_CH), F32)
        start = u0_ref[0:BF16_ROWS, 0:LANES].astype(BF16)
        for c in range(u0_ref.shape[0] // CONV_ROWS):
            conv_chunk_of(u0_ref, zeros, c, start)

    tile = jnp.minimum(i + 1, n_tiles - 1)
    starts_seq = lax.rem(tile, tiles_per_seq) == 0
    halo = jnp.where(starts_seq, 0.0, halo_ref[...])
    conv_chunk = functools.partial(conv_chunk_of, u_ref, halo)

    h = (x_ref[...]
         + _dot(conv_ref[...], wout_ref[:CONV_CH, :])
         + _dot(a_ref[...], wout_ref[CONV_CH:, :]))
    hn = _rmsnorm(h, gf_ref[...]).astype(BF16)
    n_chunks = u_ref.shape[0] // CONV_ROWS
    bounds = [0, (4 * n_chunks) // 8, (7 * n_chunks) // 8, n_chunks]

    def conv_group(q, x, not_before):
        for c in range(bounds[q], bounds[q + 1]):
            y = conv_chunk(c, not_before)
            x = _ordered_after(x, y)
            not_before = y
        return x

    gt = _dot(hn, wup_ref[:, :D_FF])
    hn = conv_group(0, hn, hn)
    up = _dot(hn, wup_ref[:, D_FF:])
    act = conv_group(1, (jax.nn.silu(gt) * up).astype(BF16), hn)
    h = h + _dot(act, wdn_ref[...])
    hn = conv_group(2, _rmsnorm(h, gp_ref[...]).astype(BF16), act)
    gate = jax.nn.sigmoid(_dot(hn, wg_ref[...]))
    h = h + gate * _dot(p_ref[...].astype(BF16), wp_ref[...])
    o_ref[...] = _rmsnorm(h, gfin_ref[...])


def _tail(u2, x2, attn2, p2, seq, cw, cb, lg, lb, wout, gf, wup, wdn, gp, wg, wp, gfin):
    t = x2.shape[0]
    rows = TAIL_ROWS
    n_tiles = t // rows
    halo_per_tile = rows // CONV_PAD
    nxt = lambda i: jnp.minimum(i + 1, n_tiles - 1)
    row_spec = lambda width: pl.BlockSpec((rows, width), lambda i: (i, 0))
    full = lambda a: pl.BlockSpec(a.shape, lambda i: (0, 0), pipeline_mode=pl.Buffered(1))
    return pl.pallas_call(
        functools.partial(_tail_kernel, seq // rows),
        grid=(n_tiles,),
        in_specs=[pl.BlockSpec((rows, CONV_CH), lambda i: (0, 0)),
                  pl.BlockSpec((rows, CONV_CH), lambda i: (nxt(i), 0)),
                  pl.BlockSpec((CONV_PAD, CONV_CH), lambda i: (nxt(i) * halo_per_tile - 1, 0)),
                  row_spec(D_MODEL), row_spec(ATTN_WIDTH), row_spec(PLE_DIM),
                  full(cw), full(cb), full(lg), full(lb),
                  full(wout), full(gf), full(wup), full(wdn), full(gp), full(wg), full(wp),
                  full(gfin)],
        out_specs=row_spec(D_MODEL),
        out_shape=jax.ShapeDtypeStruct((t, D_MODEL), F32),
        scratch_shapes=[pltpu.VMEM((rows, CONV_CH), BF16)],
        compiler_params=pltpu.CompilerParams(
            dimension_semantics=("arbitrary",), vmem_limit_bytes=VMEM_LIMIT),
        name="tail",
    )(u2, u2, u2, x2, attn2, p2, cw, cb, lg, lb, wout, gf, wup, wdn, gp, wg, wp, gfin)


def kernel(x, p, positions, norm_mix_g, w_in, conv_w, conv_b, conv_ln_g, conv_ln_b, w_out,
           norm_ffn_g, w_ffn_up, w_ffn_down, norm_ple_g, w_ple_gate, w_ple_proj, final_norm_g):
    bsz, seq, _ = x.shape
    t = bsz * seq
    depth = w_in.shape[0]
    row = lambda a: a.reshape(1, -1)
    inv_freq = ROPE_THETA ** (-jnp.arange(0, ROPE_DIM, 2, dtype=F32) / ROPE_DIM)
    h = x.reshape(t, D_MODEL)
    for i in range(depth):
        u, q, k, v = _in_proj(h, positions, inv_freq, row(norm_mix_g[i]), w_in[i])
        attn, (wout, wup, wdn, wg, wp) = _attn(
            q, k, v, bsz, seq,
            [w_out[i], w_ffn_up[i], w_ffn_down[i], w_ple_gate[i], w_ple_proj[i]])
        assert depth == 1
        h = _tail(u, h, attn, p[i].reshape(t, PLE_DIM), seq, conv_w[i], row(conv_b[i]),
                  row(conv_ln_g[i]), row(conv_ln_b[i]), wout,
                  row(norm_ffn_g[i]), wup, wdn, row(norm_ple_g[i]), wg, wp,
                  row(final_norm_g))
    return h.reshape(bsz, seq, D_MODEL)
```

```python
import functools

import jax
import jax.numpy as jnp
from jax import lax
from jax.experimental import pallas as pl
from jax.experimental.pallas import tpu as pltpu

D_MODEL = 1024
CONV_CH = 512
ATTN_HEADS = 8
HEAD_DIM = 64
ATTN_WIDTH = ATTN_HEADS * HEAD_DIM
IN_WIDTH = 2 * CONV_CH + 3 * ATTN_WIDTH
CONV_KERNEL = 31
MOBA_BLOCK = 256
MOBA_TOPK = 3
ROPE_THETA = 500000.0
ROPE_DIM = HEAD_DIM // 4
ROPE_HALF = ROPE_DIM // 2
D_FF = -(-8 * D_MODEL // (3 * 256)) * 256
PLE_DIM = 256
EPS = 1e-6

LANES = 128
SUBLANES = 8
BF16_ROWS = 16
LOG2E = 1.4426950408889634
HEADS_PER_STEP = LANES // HEAD_DIM
CONV_PAD = 32
CONV_ROWS = 32
IN_ROWS = 1024
IN_PIECE = 256
TAIL_ROWS = 512
VMEM_LIMIT = 56 * 1024 * 1024

F32 = jnp.float32
BF16 = jnp.bfloat16
NEG = float(jnp.finfo(jnp.float32).min)
POS = float(jnp.finfo(jnp.float32).max)


def _rmsnorm(x, g):
    return x * lax.rsqrt(jnp.mean(x * x, axis=-1, keepdims=True) + EPS) * g


def _dot(a, b):
    return jnp.dot(a, b, preferred_element_type=F32)


def _dot_nt(a, b):
    return lax.dot_general(a, b, (((1,), (1,)), ((), ())), preferred_element_type=F32)


def _rope_expand_matrix():
    r = jnp.arange(LANES)[:, None]
    l = jnp.arange(2 * LANES)[None, :]
    k = r % (2 * ROPE_HALF)
    piece_ok = r < 3 * 2 * ROPE_HALF
    d = (l % LANES) % HEAD_DIM
    want = d % ROPE_HALF + ROPE_HALF * (l // LANES)
    return (piece_ok & (d < ROPE_DIM) & (k == want)).astype(BF16)


def _in_proj_kernel(x_ref, pos_ref, freq_ref, expand_ref, g_ref, w32_ref,
                    u_ref, q_ref, k_ref, v_ref, w_ref):
    @pl.when(pl.program_id(0) == 0)
    def _():
        w_ref[...] = w32_ref[...].astype(BF16)

    rows = x_ref.shape[0]
    piece = IN_PIECE
    assert rows % piece == 0

    steps_per_seq = pl.num_programs(0) // pos_ref.shape[0]
    pos = pos_ref[pl.ds(pl.program_id(0) // steps_per_seq, 1), :]
    ang = pos.astype(F32) * freq_ref[...]
    small = jnp.concatenate([jnp.cos(ang), jnp.sin(ang)], axis=0)
    hi = small.astype(BF16).astype(F32)
    mid = (small - hi).astype(BF16).astype(F32)
    lo = (small - hi - mid).astype(BF16).astype(F32)
    pad = jnp.zeros((LANES - 3 * small.shape[0], small.shape[1]), F32)
    pieces = jnp.concatenate([hi, mid, lo, pad], axis=0).T.astype(BF16)
    table = _dot(pieces, expand_ref[...])
    d = lax.broadcasted_iota(jnp.int32, (1, LANES), 1) & (HEAD_DIM - 1)
    cos = table[:, :LANES] + jnp.where(d < ROPE_DIM, 0.0, 1.0)
    sin = table[:, LANES:]
    sin_from_hi = jnp.where(d < ROPE_HALF, -sin, 0.0)
    sin_from_lo = jnp.where(d >= ROPE_HALF, sin, 0.0)

    q0 = 2 * CONV_CH
    k0 = q0 + ATTN_WIDTH
    v0 = k0 + ATTN_WIDTH
    scale = HEAD_DIM ** -0.5 * LOG2E

    def epilogue(r, z_u, z_qkv):
        rs = slice(r * piece, (r + 1) * piece)
        u = z_u[:, :CONV_CH] * jax.nn.sigmoid(z_u[:, CONV_CH:])
        u_ref[rs, :] = u
        made = [u]

        def rope(t):
            return (t * cos[rs]
                    + pltpu.roll(t, ROPE_HALF, 1) * sin_from_lo[rs]
                    + pltpu.roll(t, LANES - ROPE_HALF, 1) * sin_from_hi[rs])

        for c in range(ATTN_WIDTH // LANES):
            sl = slice(c * LANES, (c + 1) * LANES)
            qc = rope(z_qkv[:, c * LANES:(c + 1) * LANES]) * scale
            kc = rope(z_qkv[:, ATTN_WIDTH + c * LANES:ATTN_WIDTH + (c + 1) * LANES])
            q_ref[rs, sl] = qc.astype(BF16)
            k_ref[rs, sl] = kc.astype(BF16)
            made += [qc, kc]
        v_ref[rs, :] = z_qkv[:, 2 * ATTN_WIDTH:].astype(BF16)
        return made

    norm = lambda r: _rmsnorm(x_ref[r * piece:(r + 1) * piece, :], g_ref[...]).astype(BF16)
    prev = None
    for r in range(rows // piece):
        hn = norm(r)
        z_u = _dot(hn, w_ref[:, :q0])
        if prev is not None:
            made = epilogue(r - 1, *prev)
            parts = [m[a:a + SUBLANES, c:c + LANES] for m in made
                     for a in range(0, piece, SUBLANES) for c in range(0, m.shape[1], LANES)]
            while len(parts) > 1:
                pairs = [jnp.maximum(parts[n], parts[n + 1]) for n in range(0, len(parts) - 1, 2)]
                parts = pairs + parts[len(parts) - len(parts) % 2:]
            zero = parts[0] - parts[0]
            head = hn[:BF16_ROWS, :LANES] + jnp.concatenate([zero, zero], axis=0).astype(BF16)
            top = jnp.concatenate([head, hn[:BF16_ROWS, LANES:]], axis=1)
            hn = jnp.concatenate([top, hn[BF16_ROWS:]], axis=0)
        prev = (z_u, _dot(hn, w_ref[:, q0:]))
    epilogue(rows // piece - 1, *prev)


def _in_proj(x2, positions, inv_freq, g, w):
    t = x2.shape[0]
    rows = IN_ROWS
    bsz, seq = positions.shape
    steps_per_seq = seq // rows
    freq = jnp.broadcast_to(inv_freq[:, None], (ROPE_HALF, rows))
    expand = _rope_expand_matrix()
    row_spec = lambda width: pl.BlockSpec((rows, width), lambda i: (i, 0))
    full = lambda a: pl.BlockSpec(a.shape, lambda i: (0, 0))
    return pl.pallas_call(
        _in_proj_kernel,
        grid=(t // rows,),
        in_specs=[row_spec(D_MODEL),
                  pl.BlockSpec((bsz, rows), lambda i: (0, i % steps_per_seq)),
                  full(freq), full(expand), full(g),
                  pl.BlockSpec(w.shape, lambda i: (0, 0), pipeline_mode=pl.Buffered(1))],
        out_specs=[row_spec(CONV_CH), row_spec(ATTN_WIDTH), row_spec(ATTN_WIDTH), row_spec(ATTN_WIDTH)],
        out_shape=[jax.ShapeDtypeStruct((t, CONV_CH), F32),
                   jax.ShapeDtypeStruct((t, ATTN_WIDTH), BF16),
                   jax.ShapeDtypeStruct((t, ATTN_WIDTH), BF16),
                   jax.ShapeDtypeStruct((t, ATTN_WIDTH), BF16)],
        scratch_shapes=[pltpu.VMEM(w.shape, BF16)],
        compiler_params=pltpu.CompilerParams(
            dimension_semantics=("arbitrary",), vmem_limit_bytes=VMEM_LIMIT),
        name="in_proj",
    )(x2, positions, freq, expand, g, w)


def _conv_rows(window, w_ref, b_ref, lg_ref, lb_ref, not_before):
    first = CONV_PAD - (CONV_KERNEL - 1)
    win_rows = CONV_ROWS + CONV_PAD
    a = not_before[:BF16_ROWS, :LANES].astype(F32)[0:1, :]
    hold = a - a
    accs = []
    for g in range(CONV_CH // LANES):
        lanes = slice(g * LANES, (g + 1) * LANES)
        win = window(lanes)
        acc = jnp.broadcast_to(b_ref[:, lanes] + hold, (CONV_ROWS, LANES))
        for phase in range(SUBLANES):
            shifted = win if phase == 0 else pltpu.roll(win, win_rows - phase, 0)
            for base in range(0, CONV_PAD + 1, SUBLANES):
                j = base + phase - first
                if 0 <= j < CONV_KERNEL:
                    acc = acc + w_ref[j:j + 1, lanes] * shifted[base:base + CONV_ROWS]
        accs.append(acc)
        hold = acc[0:1, :] - acc[0:1, :]
    acc = jnp.concatenate(accs, axis=1)
    mu = jnp.mean(acc, axis=-1, keepdims=True)
    dev = acc - mu
    var = jnp.mean(dev * dev, axis=-1, keepdims=True)
    y = dev * lax.rsqrt(var + EPS) * lg_ref[...] + lb_ref[...]
    return (y * jax.nn.sigmoid(y)).astype(BF16)


def _attn_kernel(n_weights, q_ref, k_ref, v_ref, *refs):
    w32_refs = refs[:n_weights]
    o_ref = refs[n_weights]
    w16_refs = refs[n_weights + 1:]
    for src, dst in zip(w32_refs, w16_refs):
        dst[...] = src[...].astype(BF16)

    seq = q_ref.shape[0]
    nb = seq // MOBA_BLOCK
    q = q_ref[...]
    k = k_ref[...]
    vt = v_ref[...].astype(F32).T.astype(BF16)
    lane = lax.broadcasted_iota(jnp.int32, (1, LANES), 1)
    kmean = jnp.mean(k.astype(F32).reshape(nb, MOBA_BLOCK, LANES), axis=1)
    heads = HEADS_PER_STEP
    cols = heads * MOBA_BLOCK
    key_row = lax.broadcasted_iota(jnp.int32, (MOBA_BLOCK, cols), 0)
    q_col = lax.broadcasted_iota(jnp.int32, (MOBA_BLOCK, cols), 1) & (MOBA_BLOCK - 1)
    causal = key_row <= q_col
    blk = lax.broadcasted_iota(jnp.int32, (nb, MOBA_BLOCK), 0)
    first_ranked = MOBA_TOPK + 1
    vt_aug = jnp.concatenate([vt, jnp.ones((BF16_ROWS, seq), BF16)], axis=0)

    in_head = [(lane >= h * HEAD_DIM) & (lane < (h + 1) * HEAD_DIM) for h in range(heads)]
    gate_t = [_dot_nt(jnp.where(m, kmean, 0.0).astype(BF16), q[first_ranked * MOBA_BLOCK:])
              for m in in_head]

    def scores(i):
        qi = q[i * MOBA_BLOCK:(i + 1) * MOBA_BLOCK]
        q2 = jnp.concatenate([jnp.where(m, qi, jnp.zeros_like(qi)) for m in in_head], axis=0)
        return _dot_nt(k[:(i + 1) * MOBA_BLOCK], q2)

    def finish(i, st):
        if i >= first_ranked:
            caps = []
            for h in range(heads):
                c0 = (i - first_ranked) * MOBA_BLOCK
                gi = gate_t[h][:, c0:c0 + MOBA_BLOCK]
                rank = jnp.zeros((nb, MOBA_BLOCK), jnp.int32)
                for m in range(i):
                    gm = gi[m:m + 1, :]
                    beats = (gm > gi) | ((gm == gi) & (blk > m))
                    rank = rank + jnp.where(beats, 1, 0)
                caps.append(jnp.where(rank < MOBA_TOPK, POS, NEG))
            cap = jnp.concatenate(caps, axis=1)
        pieces = []
        for j in range(i + 1):
            sj = st[j * MOBA_BLOCK:(j + 1) * MOBA_BLOCK]
            if j == i:
                sj = jnp.where(causal, sj, NEG)
            elif i >= first_ranked:
                sj = jnp.minimum(sj, cap[j:j + 1, :])
            pieces.append(sj)
        st = pieces[0] if i == 0 else jnp.concatenate(pieces, axis=0)
        m_col = jnp.max(st, axis=0, keepdims=True)
        p = jnp.exp2(st - m_col).astype(BF16)
        ot = _dot(vt_aug[:, :(i + 1) * MOBA_BLOCK], p)
        denom = ot[LANES:LANES + 1]
        out_t = jnp.concatenate(
            [ot[h * HEAD_DIM:(h + 1) * HEAD_DIM, h * MOBA_BLOCK:(h + 1) * MOBA_BLOCK]
             / denom[:, h * MOBA_BLOCK:(h + 1) * MOBA_BLOCK] for h in range(heads)], axis=0)
        o_ref[i * MOBA_BLOCK:(i + 1) * MOBA_BLOCK, :] = out_t.T.astype(BF16)

    st_next = scores(0)
    for i in range(nb):
        st = st_next
        if i + 1 < nb:
            st_next = scores(i + 1)
        finish(i, st)


def _attn(q2, k2, v2, bsz, seq, weights):
    pairs = ATTN_WIDTH // LANES
    n_steps = bsz * pairs
    spec = pl.BlockSpec((seq, LANES), lambda b, hp: (b, hp))

    def slice_spec(w):
        n = next(n for n in (n_steps, n_steps // 2, n_steps // 4, 1)
                 if w.shape[0] % n == 0 and (w.shape[0] // n) % BF16_ROWS == 0)
        return pl.BlockSpec((w.shape[0] // n, w.shape[1]),
                            lambda b, hp: (jnp.minimum(b * pairs + hp, n - 1), 0))

    w_specs = [slice_spec(w) for w in weights]
    out = pl.pallas_call(
        functools.partial(_attn_kernel, len(weights)),
        grid=(bsz, pairs),
        in_specs=[spec, spec, spec] + w_specs,
        out_specs=[spec] + w_specs,
        out_shape=[jax.ShapeDtypeStruct(q2.shape, BF16)]
        + [jax.ShapeDtypeStruct(w.shape, BF16) for w in weights],
        compiler_params=pltpu.CompilerParams(
            dimension_semantics=("arbitrary", "arbitrary"), vmem_limit_bytes=VMEM_LIMIT),
        name="attn",
    )(q2, k2, v2, *weights)
    return out[0], out[1:]


def _ordered_after(x, anchor):
    rows, width = anchor.shape
    parts = [anchor[r:r + BF16_ROWS, c:c + LANES]
             for r in range(0, rows, BF16_ROWS) for c in range(0, width, LANES)]
    while len(parts) > 1:
        parts = [jnp.maximum(parts[n], parts[n + 1]) for n in range(0, len(parts), 2)]
    a = parts[0]
    head = x[:BF16_ROWS, :LANES] + (a - a)
    top = jnp.concatenate([head, x[:BF16_ROWS, LANES:]], axis=1)
    return jnp.concatenate([top, x[BF16_ROWS:]], axis=0)


def _tail_kernel(tiles_per_seq, u0_ref, u_ref, halo_ref, x_ref, a_ref, p_ref, cw_ref, cb_ref, lg_ref,
                 lb_ref, wout_ref, gf_ref, wup_ref, wdn_ref, gp_ref, wg_ref, wp_ref, gfin_ref,
                 o_ref, conv_ref):
    i = pl.program_id(0)
    n_tiles = pl.num_programs(0)

    def conv_chunk_of(tile_ref, halo, c, not_before):
        def window(lanes):
            start = c * CONV_ROWS - CONV_PAD
            stop = (c + 1) * CONV_ROWS
            if start < 0:
                return jnp.concatenate([halo[CONV_PAD + start:, lanes], tile_ref[0:stop, lanes]],
                                       axis=0)
            return tile_ref[start:stop, lanes]
        y = _conv_rows(window, cw_ref, cb_ref, lg_ref, lb_ref, not_before)
        conv_ref[c * CONV_ROWS:(c + 1) * CONV_ROWS, :] = y
        return y

    @pl.when(i == 0)
    def _():
        zeros = jnp.zeros((CONV_PAD, CONV_CH), F32)
        start = u0_ref[0:BF16_ROWS, 0:LANES].astype(BF16)
        for c in range(u0_ref.shape[0] // CONV_ROWS):
            conv_chunk_of(u0_ref, zeros, c, start)

    tile = jnp.minimum(i + 1, n_tiles - 1)
    starts_seq = lax.rem(tile, tiles_per_seq) == 0
    halo = jnp.where(starts_seq, 0.0, halo_ref[...])
    conv_chunk = functools.partial(conv_chunk_of, u_ref, halo)

    h = (x_ref[...]
         + _dot(conv_ref[...], wout_ref[:CONV_CH, :])
         + _dot(a_ref[...], wout_ref[CONV_CH:, :]))
    hn = _rmsnorm(h, gf_ref[...]).astype(BF16)
    n_chunks = u_ref.shape[0] // CONV_ROWS
    bounds = [0, (4 * n_chunks) // 8, (7 * n_chunks) // 8, n_chunks]

    def conv_group(q, x, not_before):
        for c in range(bounds[q], bounds[q + 1]):
            y = conv_chunk(c, not_before)
            x = _ordered_after(x, y)
            not_before = y
        return x

    gt = _dot(hn, wup_ref[:, :D_FF])
    hn = conv_group(0, hn, hn)
    up = _dot(hn, wup_ref[:, D_FF:])
    act = conv_group(1, (jax.nn.silu(gt) * up).astype(BF16), hn)
    h = h + _dot(act, wdn_ref[...])
    hn = conv_group(2, _rmsnorm(h, gp_ref[...]).astype(BF16), act)
    gate = jax.nn.sigmoid(_dot(hn, wg_ref[...]))
    h = h + gate * _dot(p_ref[...].astype(BF16), wp_ref[...])
    o_ref[...] = _rmsnorm(h, gfin_ref[...])


def _tail(u2, x2, attn2, p2, seq, cw, cb, lg, lb, wout, gf, wup, wdn, gp, wg, wp, gfin):
    t = x2.shape[0]
    rows = TAIL_ROWS
    n_tiles = t // rows
    halo_per_tile = rows // CONV_PAD
    nxt = lambda i: jnp.minimum(i + 1, n_tiles - 1)
    row_spec = lambda width: pl.BlockSpec((rows, width), lambda i: (i, 0))
    full = lambda a: pl.BlockSpec(a.shape, lambda i: (0, 0), pipeline_mode=pl.Buffered(1))
    return pl.pallas_call(
        functools.partial(_tail_kernel, seq // rows),
        grid=(n_tiles,),
        in_specs=[pl.BlockSpec((rows, CONV_CH), lambda i: (0, 0)),
                  pl.BlockSpec((rows, CONV_CH), lambda i: (nxt(i), 0)),
                  pl.BlockSpec((CONV_PAD, CONV_CH), lambda i: (nxt(i) * halo_per_tile - 1, 0)),
                  row_spec(D_MODEL), row_spec(ATTN_WIDTH), row_spec(PLE_DIM),
                  full(cw), full(cb), full(lg), full(lb),
                  full(wout), full(gf), full(wup), full(wdn), full(gp), full(wg), full(wp),
                  full(gfin)],
        out_specs=row_spec(D_MODEL),
        out_shape=jax.ShapeDtypeStruct((t, D_MODEL), F32),
        scratch_shapes=[pltpu.VMEM((rows, CONV_CH), BF16)],
        compiler_params=pltpu.CompilerParams(
            dimension_semantics=("arbitrary",), vmem_limit_bytes=VMEM_LIMIT),
        name="tail",
    )(u2, u2, u2, x2, attn2, p2, cw, cb, lg, lb, wout, gf, wup, wdn, gp, wg, wp, gfin)


def kernel(x, p, positions, norm_mix_g, w_in, conv_w, conv_b, conv_ln_g, conv_ln_b, w_out,
           norm_ffn_g, w_ffn_up, w_ffn_down, norm_ple_g, w_ple_gate, w_ple_proj, final_norm_g):
    bsz, seq, _ = x.shape
    t = bsz * seq
    depth = w_in.shape[0]
    row = lambda a: a.reshape(1, -1)
    inv_freq = ROPE_THETA ** (-jnp.arange(0, ROPE_DIM, 2, dtype=F32) / ROPE_DIM)
    h = x.reshape(t, D_MODEL)
    for i in range(depth):
        u, q, k, v = _in_proj(h, positions, inv_freq, row(norm_mix_g[i]), w_in[i])
        attn, (wout, wup, wdn, wg, wp) = _attn(
            q, k, v, bsz, seq,
            [w_out[i], w_ffn_up[i], w_ffn_down[i], w_ple_gate[i], w_ple_proj[i]])
        assert depth == 1
        h = _tail(u, h, attn, p[i].reshape(t, PLE_DIM), seq, conv_w[i], row(conv_b[i]),
                  row(conv_ln_g[i]), row(conv_ln_b[i]), wout,
                  row(norm_ffn_g[i]), wup, wdn, row(norm_ple_g[i]), wg, wp,
                  row(final_norm_g))
    return h.reshape(bsz, seq, D_MODEL)
```

```python
import functools

import jax
import jax.numpy as jnp
from jax import lax
from jax.experimental import pallas as pl
from jax.experimental.pallas import tpu as pltpu

D_MODEL = 1024
CONV_CH = 512
ATTN_HEADS = 8
HEAD_DIM = 64
ATTN_WIDTH = ATTN_HEADS * HEAD_DIM
IN_WIDTH = 2 * CONV_CH + 3 * ATTN_WIDTH
CONV_KERNEL = 31
MOBA_BLOCK = 256
MOBA_TOPK = 3
ROPE_THETA = 500000.0
ROPE_DIM = HEAD_DIM // 4
ROPE_HALF = ROPE_DIM // 2
D_FF = -(-8 * D_MODEL // (3 * 256)) * 256
PLE_DIM = 256
EPS = 1e-6

LANES = 128
SUBLANES = 8
BF16_ROWS = 16
LOG2E = 1.4426950408889634
HEADS_PER_STEP = LANES // HEAD_DIM
CONV_PAD = 32
CONV_ROWS = 32
IN_ROWS = 1024
IN_PIECE = 256
TAIL_ROWS = 512
ATTN_PAIRS = 2
VMEM_LIMIT = 56 * 1024 * 1024

F32 = jnp.float32
BF16 = jnp.bfloat16
NEG = float(jnp.finfo(jnp.float32).min)
POS = float(jnp.finfo(jnp.float32).max)


def _rmsnorm(x, g):
    return x * lax.rsqrt(jnp.mean(x * x, axis=-1, keepdims=True) + EPS) * g


def _dot(a, b):
    return jnp.dot(a, b, preferred_element_type=F32)


def _dot_nt(a, b):
    return lax.dot_general(a, b, (((1,), (1,)), ((), ())), preferred_element_type=F32)


def _rope_expand_matrix():
    r = jnp.arange(LANES)[:, None]
    l = jnp.arange(2 * LANES)[None, :]
    k = r % (2 * ROPE_HALF)
    piece_ok = r < 3 * 2 * ROPE_HALF
    d = (l % LANES) % HEAD_DIM
    want = d % ROPE_HALF + ROPE_HALF * (l // LANES)
    return (piece_ok & (d < ROPE_DIM) & (k == want)).astype(BF16)


def _in_proj_kernel(x_ref, pos_ref, freq_ref, expand_ref, g_ref, w32_ref,
                    u_ref, q_ref, k_ref, v_ref, w_ref):
    @pl.when(pl.program_id(0) == 0)
    def _():
        w_ref[...] = w32_ref[...].astype(BF16)

    rows = x_ref.shape[0]
    piece = IN_PIECE
    assert rows % piece == 0

    ang = pos_ref[0].astype(F32) * freq_ref[...]
    small = jnp.concatenate([jnp.cos(ang), jnp.sin(ang)], axis=0)
    hi = small.astype(BF16).astype(F32)
    mid = (small - hi).astype(BF16).astype(F32)
    lo = (small - hi - mid).astype(BF16).astype(F32)
    pad = jnp.zeros((LANES - 3 * small.shape[0], small.shape[1]), F32)
    pieces = jnp.concatenate([hi, mid, lo, pad], axis=0).T.astype(BF16)
    table = _dot(pieces, expand_ref[...])
    d = lax.broadcasted_iota(jnp.int32, (1, LANES), 1) & (HEAD_DIM - 1)
    cos = table[:, :LANES] + jnp.where(d < ROPE_DIM, 0.0, 1.0)
    sin = table[:, LANES:]
    sin_from_hi = jnp.where(d < ROPE_HALF, -sin, 0.0)
    sin_from_lo = jnp.where(d >= ROPE_HALF, sin, 0.0)

    q0 = 2 * CONV_CH
    k0 = q0 + ATTN_WIDTH
    v0 = k0 + ATTN_WIDTH
    scale = HEAD_DIM ** -0.5 * LOG2E

    def epilogue(r, z_u, z_qkv):
        rs = slice(r * piece, (r + 1) * piece)
        u = z_u[:, :CONV_CH] * jax.nn.sigmoid(z_u[:, CONV_CH:])
        u_ref[rs, :] = u
        made = [u]

        def rope(t):
            return (t * cos[rs]
                    + pltpu.roll(t, ROPE_HALF, 1) * sin_from_lo[rs]
                    + pltpu.roll(t, LANES - ROPE_HALF, 1) * sin_from_hi[rs])

        for c in range(ATTN_WIDTH // LANES):
            sl = slice(c * LANES, (c + 1) * LANES)
            qc = rope(z_qkv[:, c * LANES:(c + 1) * LANES]) * scale
            kc = rope(z_qkv[:, ATTN_WIDTH + c * LANES:ATTN_WIDTH + (c + 1) * LANES])
            q_ref[rs, sl] = qc.astype(BF16)
            k_ref[rs, sl] = kc.astype(BF16)
            made += [qc, kc]
        v_ref[rs, :] = z_qkv[:, 2 * ATTN_WIDTH:].astype(BF16)
        return made

    norm = lambda r: _rmsnorm(x_ref[r * piece:(r + 1) * piece, :], g_ref[...]).astype(BF16)
    prev = None
    for r in range(rows // piece):
        hn = norm(r)
        z_u = _dot(hn, w_ref[:, :q0])
        if prev is not None:
            made = epilogue(r - 1, *prev)
            parts = [m[a:a + SUBLANES, c:c + LANES] for m in made
                     for a in range(0, piece, SUBLANES) for c in range(0, m.shape[1], LANES)]
            while len(parts) > 1:
                pairs = [jnp.maximum(parts[n], parts[n + 1]) for n in range(0, len(parts) - 1, 2)]
                parts = pairs + parts[len(parts) - len(parts) % 2:]
            zero = parts[0] - parts[0]
            head = hn[:BF16_ROWS, :LANES] + jnp.concatenate([zero, zero], axis=0).astype(BF16)
            top = jnp.concatenate([head, hn[:BF16_ROWS, LANES:]], axis=1)
            hn = jnp.concatenate([top, hn[BF16_ROWS:]], axis=0)
        prev = (z_u, _dot(hn, w_ref[:, q0:]))
    epilogue(rows // piece - 1, *prev)


def _in_proj(x2, positions, inv_freq, g, w):
    t = x2.shape[0]
    rows = IN_ROWS
    pos3 = positions.reshape(t // rows, 1, rows)
    freq = jnp.broadcast_to(inv_freq[:, None], (ROPE_HALF, rows))
    expand = _rope_expand_matrix()
    row_spec = lambda width: pl.BlockSpec((rows, width), lambda i: (i, 0))
    full = lambda a: pl.BlockSpec(a.shape, lambda i: (0, 0))
    return pl.pallas_call(
        _in_proj_kernel,
        grid=(t // rows,),
        in_specs=[row_spec(D_MODEL), pl.BlockSpec((1, 1, rows), lambda i: (i, 0, 0)),
                  full(freq), full(expand), full(g),
                  pl.BlockSpec(w.shape, lambda i: (0, 0), pipeline_mode=pl.Buffered(1))],
        out_specs=[row_spec(CONV_CH), row_spec(ATTN_WIDTH), row_spec(ATTN_WIDTH), row_spec(ATTN_WIDTH)],
        out_shape=[jax.ShapeDtypeStruct((t, CONV_CH), F32),
                   jax.ShapeDtypeStruct((t, ATTN_WIDTH), BF16),
                   jax.ShapeDtypeStruct((t, ATTN_WIDTH), BF16),
                   jax.ShapeDtypeStruct((t, ATTN_WIDTH), BF16)],
        scratch_shapes=[pltpu.VMEM(w.shape, BF16)],
        compiler_params=pltpu.CompilerParams(
            dimension_semantics=("arbitrary",), vmem_limit_bytes=VMEM_LIMIT),
        name="in_proj",
    )(x2, pos3, freq, expand, g, w)


def _conv_rows(window, w_ref, b_ref, lg_ref, lb_ref, not_before):
    first = CONV_PAD - (CONV_KERNEL - 1)
    win_rows = CONV_ROWS + CONV_PAD
    a = not_before[:BF16_ROWS, :LANES].astype(F32)[0:1, :]
    hold = a - a
    accs = []
    for g in range(CONV_CH // LANES):
        lanes = slice(g * LANES, (g + 1) * LANES)
        win = window(lanes)
        acc = jnp.broadcast_to(b_ref[:, lanes] + hold, (CONV_ROWS, LANES))
        for phase in range(SUBLANES):
            shifted = win if phase == 0 else pltpu.roll(win, win_rows - phase, 0)
            for base in range(0, CONV_PAD + 1, SUBLANES):
                j = base + phase - first
                if 0 <= j < CONV_KERNEL:
                    acc = acc + w_ref[j:j + 1, lanes] * shifted[base:base + CONV_ROWS]
        accs.append(acc)
        hold = acc[0:1, :] - acc[0:1, :]
    acc = jnp.concatenate(accs, axis=1)
    mu = jnp.mean(acc, axis=-1, keepdims=True)
    dev = acc - mu
    var = jnp.mean(dev * dev, axis=-1, keepdims=True)
    y = dev * lax.rsqrt(var + EPS) * lg_ref[...] + lb_ref[...]
    return (y * jax.nn.sigmoid(y)).astype(BF16)


def _attn_kernel(n_weights, q_ref, k_ref, v_ref, *refs):
    w32_refs = refs[:n_weights]
    o_ref = refs[n_weights]
    w16_refs = refs[n_weights + 1:]
    for src, dst in zip(w32_refs, w16_refs):
        dst[...] = src[...].astype(BF16)

    seq = q_ref.shape[0]
    nb = seq // MOBA_BLOCK
    lane = lax.broadcasted_iota(jnp.int32, (1, LANES), 1)
    heads = HEADS_PER_STEP
    cols = heads * MOBA_BLOCK
    key_row = lax.broadcasted_iota(jnp.int32, (MOBA_BLOCK, cols), 0)
    q_col = lax.broadcasted_iota(jnp.int32, (MOBA_BLOCK, cols), 1) & (MOBA_BLOCK - 1)
    causal = key_row <= q_col
    blk = lax.broadcasted_iota(jnp.int32, (nb, MOBA_BLOCK), 0)
    first_ranked = MOBA_TOPK + 1
    in_head = [(lane >= h * HEAD_DIM) & (lane < (h + 1) * HEAD_DIM) for h in range(heads)]

    pairs = []
    for pr in range(q_ref.shape[1] // LANES):
        ls = slice(pr * LANES, (pr + 1) * LANES)
        q = q_ref[:, ls]
        k = k_ref[:, ls]
        vt = v_ref[:, ls].astype(F32).T.astype(BF16)
        kmean = jnp.mean(k.astype(F32).reshape(nb, MOBA_BLOCK, LANES), axis=1)
        vt_aug = jnp.concatenate([vt, jnp.ones((BF16_ROWS, seq), BF16)], axis=0)
        gate_t = [_dot_nt(jnp.where(m, kmean, 0.0).astype(BF16), q[first_ranked * MOBA_BLOCK:])
                  for m in in_head]
        pairs.append((q, k, vt_aug, gate_t))

    def scores(pr, i):
        q, k = pairs[pr][:2]
        qi = q[i * MOBA_BLOCK:(i + 1) * MOBA_BLOCK]
        q2 = jnp.concatenate([jnp.where(m, qi, jnp.zeros_like(qi)) for m in in_head], axis=0)
        return _dot_nt(k[:(i + 1) * MOBA_BLOCK], q2)

    def finish(pr, i, st):
        vt_aug, gate_t = pairs[pr][2:]
        if i >= first_ranked:
            caps = []
            for h in range(heads):
                c0 = (i - first_ranked) * MOBA_BLOCK
                gi = gate_t[h][:, c0:c0 + MOBA_BLOCK]
                rank = jnp.zeros((nb, MOBA_BLOCK), jnp.int32)
                for m in range(i):
                    gm = gi[m:m + 1, :]
                    beats = (gm > gi) | ((gm == gi) & (blk > m))
                    rank = rank + jnp.where(beats, 1, 0)
                caps.append(jnp.where(rank < MOBA_TOPK, POS, NEG))
            cap = jnp.concatenate(caps, axis=1)
        pieces = []
        for j in range(i + 1):
            sj = st[j * MOBA_BLOCK:(j + 1) * MOBA_BLOCK]
            if j == i:
                sj = jnp.where(causal, sj, NEG)
            elif i >= first_ranked:
                sj = jnp.minimum(sj, cap[j:j + 1, :])
            pieces.append(sj)
        st = pieces[0] if i == 0 else jnp.concatenate(pieces, axis=0)
        m_col = jnp.max(st, axis=0, keepdims=True)
        p = jnp.exp2(st - m_col).astype(BF16)
        ot = _dot(vt_aug[:, :(i + 1) * MOBA_BLOCK], p)
        denom = ot[LANES:LANES + 1]
        out_t = jnp.concatenate(
            [ot[h * HEAD_DIM:(h + 1) * HEAD_DIM, h * MOBA_BLOCK:(h + 1) * MOBA_BLOCK]
             / denom[:, h * MOBA_BLOCK:(h + 1) * MOBA_BLOCK] for h in range(heads)], axis=0)
        o_ref[i * MOBA_BLOCK:(i + 1) * MOBA_BLOCK, pr * LANES:(pr + 1) * LANES] = (
            out_t.T.astype(BF16))

    units = [(pr, i) for pr in range(len(pairs)) for i in range(nb)]
    st_next = scores(*units[0])
    for n, unit in enumerate(units):
        st = st_next
        if n + 1 < len(units):
            st_next = scores(*units[n + 1])
        finish(*unit, st)


def _attn(q2, k2, v2, bsz, seq, weights):
    width = ATTN_PAIRS * LANES
    pairs = ATTN_WIDTH // width
    n_steps = bsz * pairs
    spec = pl.BlockSpec((seq, width), lambda b, hp: (b, hp))

    def slice_spec(w):
        n = next(n for n in (n_steps, n_steps // 2, n_steps // 4, 1)
                 if w.shape[0] % n == 0 and (w.shape[0] // n) % BF16_ROWS == 0)
        return pl.BlockSpec((w.shape[0] // n, w.shape[1]),
                            lambda b, hp: (jnp.minimum(b * pairs + hp, n - 1), 0))

    w_specs = [slice_spec(w) for w in weights]
    out = pl.pallas_call(
        functools.partial(_attn_kernel, len(weights)),
        grid=(bsz, pairs),
        in_specs=[spec, spec, spec] + w_specs,
        out_specs=[spec] + w_specs,
        out_shape=[jax.ShapeDtypeStruct(q2.shape, BF16)]
        + [jax.ShapeDtypeStruct(w.shape, BF16) for w in weights],
        compiler_params=pltpu.CompilerParams(
            dimension_semantics=("arbitrary", "arbitrary"), vmem_limit_bytes=VMEM_LIMIT),
        name="attn",
    )(q2, k2, v2, *weights)
    return out[0], out[1:]


def _ordered_after(x, anchor):
    rows, width = anchor.shape
    parts = [anchor[r:r + BF16_ROWS, c:c + LANES]
             for r in range(0, rows, BF16_ROWS) for c in range(0, width, LANES)]
    while len(parts) > 1:
        parts = [jnp.maximum(parts[n], parts[n + 1]) for n in range(0, len(parts), 2)]
    a = parts[0]
    head = x[:BF16_ROWS, :LANES] + (a - a)
    top = jnp.concatenate([head, x[:BF16_ROWS, LANES:]], axis=1)
    return jnp.concatenate([top, x[BF16_ROWS:]], axis=0)


def _tail_kernel(tiles_per_seq, u0_ref, u_ref, halo_ref, x_ref, a_ref, p_ref, cw_ref, cb_ref, lg_ref,
                 lb_ref, wout_ref, gf_ref, wup_ref, wdn_ref, gp_ref, wg_ref, wp_ref, gfin_ref,
                 o_ref, conv_ref):
    i = pl.program_id(0)
    n_tiles = pl.num_programs(0)

    def conv_chunk_of(tile_ref, halo, c, not_before):
        def window(lanes):
            start = c * CONV_ROWS - CONV_PAD
            stop = (c + 1) * CONV_ROWS
            if start < 0:
                return jnp.concatenate([halo[CONV_PAD + start:, lanes], tile_ref[0:stop, lanes]],
                                       axis=0)
            return tile_ref[start:stop, lanes]
        y = _conv_rows(window, cw_ref, cb_ref, lg_ref, lb_ref, not_before)
        conv_ref[c * CONV_ROWS:(c + 1) * CONV_ROWS, :] = y
        return y

    @pl.when(i == 0)
    def _():
        zeros = jnp.zeros((CONV_PAD, CONV_CH), F32)
        start = u0_ref[0:BF16_ROWS, 0:LANES].astype(BF16)
        for c in range(u0_ref.shape[0] // CONV_ROWS):
            conv_chunk_of(u0_ref, zeros, c, start)

    tile = jnp.minimum(i + 1, n_tiles - 1)
    starts_seq = lax.rem(tile, tiles_per_seq) == 0
    halo = jnp.where(starts_seq, 0.0, halo_ref[...])
    conv_chunk = functools.partial(conv_chunk_of, u_ref, halo)

    h = (x_ref[...]
         + _dot(conv_ref[...], wout_ref[:CONV_CH, :])
         + _dot(a_ref[...], wout_ref[CONV_CH:, :]))
    hn = _rmsnorm(h, gf_ref[...]).astype(BF16)
    n_chunks = u_ref.shape[0] // CONV_ROWS
    bounds = [0, (4 * n_chunks) // 8, (7 * n_chunks) // 8, n_chunks]

    def conv_group(q, x, not_before):
        for c in range(bounds[q], bounds[q + 1]):
            y = conv_chunk(c, not_before)
            x = _ordered_after(x, y)
            not_before = y
        return x

    gt = _dot(hn, wup_ref[:, :D_FF])
    hn = conv_group(0, hn, hn)
    up = _dot(hn, wup_ref[:, D_FF:])
    act = conv_group(1, (jax.nn.silu(gt) * up).astype(BF16), hn)
    h = h + _dot(act, wdn_ref[...])
    hn = conv_group(2, _rmsnorm(h, gp_ref[...]).astype(BF16), act)
    gate = jax.nn.sigmoid(_dot(hn, wg_ref[...]))
    h = h + gate * _dot(p_ref[...].astype(BF16), wp_ref[...])
    o_ref[...] = _rmsnorm(h, gfin_ref[...])


def _tail(u2, x2, attn2, p2, seq, cw, cb, lg, lb, wout, gf, wup, wdn, gp, wg, wp, gfin):
    t = x2.shape[0]
    rows = TAIL_ROWS
    n_tiles = t // rows
    halo_per_tile = rows // CONV_PAD
    nxt = lambda i: jnp.minimum(i + 1, n_tiles - 1)
    row_spec = lambda width: pl.BlockSpec((rows, width), lambda i: (i, 0))
    full = lambda a: pl.BlockSpec(a.shape, lambda i: (0, 0), pipeline_mode=pl.Buffered(1))
    return pl.pallas_call(
        functools.partial(_tail_kernel, seq // rows),
        grid=(n_tiles,),
        in_specs=[pl.BlockSpec((rows, CONV_CH), lambda i: (0, 0)),
                  pl.BlockSpec((rows, CONV_CH), lambda i: (nxt(i), 0)),
                  pl.BlockSpec((CONV_PAD, CONV_CH), lambda i: (nxt(i) * halo_per_tile - 1, 0)),
                  row_spec(D_MODEL), row_spec(ATTN_WIDTH), row_spec(PLE_DIM),
                  full(cw), full(cb), full(lg), full(lb),
                  full(wout), full(gf), full(wup), full(wdn), full(gp), full(wg), full(wp),
                  full(gfin)],
        out_specs=row_spec(D_MODEL),
        out_shape=jax.ShapeDtypeStruct((t, D_MODEL), F32),
        scratch_shapes=[pltpu.VMEM((rows, CONV_CH), BF16)],
        compiler_params=pltpu.CompilerParams(
            dimension_semantics=("arbitrary",), vmem_limit_bytes=VMEM_LIMIT),
        name="tail",
    )(u2, u2, u2, x2, attn2, p2, cw, cb, lg, lb, wout, gf, wup, wdn, gp, wg, wp, gfin)


def kernel(x, p, positions, norm_mix_g, w_in, conv_w, conv_b, conv_ln_g, conv_ln_b, w_out,
           norm_ffn_g, w_ffn_up, w_ffn_down, norm_ple_g, w_ple_gate, w_ple_proj, final_norm_g):
    bsz, seq, _ = x.shape
    t = bsz * seq
    depth = w_in.shape[0]
    row = lambda a: a.reshape(1, -1)
    inv_freq = ROPE_THETA ** (-jnp.arange(0, ROPE_DIM, 2, dtype=F32) / ROPE_DIM)
    h = x.reshape(t, D_MODEL)
    for i in range(depth):
        u, q, k, v = _in_proj(h, positions, inv_freq, row(norm_mix_g[i]), w_in[i])
        attn, (wout, wup, wdn, wg, wp) = _attn(
            q, k, v, bsz, seq,
            [w_out[i], w_ffn_up[i], w_ffn_down[i], w_ple_gate[i], w_ple_proj[i]])
        assert depth == 1
        h = _tail(u, h, attn, p[i].reshape(t, PLE_DIM), seq, conv_w[i], row(conv_b[i]),
                  row(conv_ln_g[i]), row(conv_ln_b[i]), wout,
                  row(norm_ffn_g[i]), wup, wdn, row(norm_ple_g[i]), wg, wp,
                  row(final_norm_g))
    return h.reshape(bsz, seq, D_MODEL)
```

```python
import functools

import jax
import jax.numpy as jnp
from jax import lax
from jax.experimental import pallas as pl
from jax.experimental.pallas import tpu as pltpu

D_MODEL = 1024
CONV_CH = 512
ATTN_HEADS = 8
HEAD_DIM = 64
ATTN_WIDTH = ATTN_HEADS * HEAD_DIM
IN_WIDTH = 2 * CONV_CH + 3 * ATTN_WIDTH
CONV_KERNEL = 31
MOBA_BLOCK = 256
MOBA_TOPK = 3
ROPE_THETA = 500000.0
ROPE_DIM = HEAD_DIM // 4
ROPE_HALF = ROPE_DIM // 2
D_FF = -(-8 * D_MODEL // (3 * 256)) * 256
PLE_DIM = 256
EPS = 1e-6

LANES = 128
SUBLANES = 8
BF16_ROWS = 16
LOG2E = 1.4426950408889634
HEADS_PER_STEP = LANES // HEAD_DIM
CONV_PAD = 32
CONV_ROWS = 32
IN_ROWS = 1024
IN_PIECE = 256
TAIL_ROWS = 512
ATTN_PAIRS = 4
VMEM_LIMIT = 56 * 1024 * 1024

F32 = jnp.float32
BF16 = jnp.bfloat16
NEG = float(jnp.finfo(jnp.float32).min)
POS = float(jnp.finfo(jnp.float32).max)


def _rmsnorm(x, g):
    return x * lax.rsqrt(jnp.mean(x * x, axis=-1, keepdims=True) + EPS) * g


def _dot(a, b):
    return jnp.dot(a, b, preferred_element_type=F32)


def _dot_nt(a, b):
    return lax.dot_general(a, b, (((1,), (1,)), ((), ())), preferred_element_type=F32)


def _rope_expand_matrix():
    r = jnp.arange(LANES)[:, None]
    l = jnp.arange(2 * LANES)[None, :]
    k = r % (2 * ROPE_HALF)
    piece_ok = r < 3 * 2 * ROPE_HALF
    d = (l % LANES) % HEAD_DIM
    want = d % ROPE_HALF + ROPE_HALF * (l // LANES)
    return (piece_ok & (d < ROPE_DIM) & (k == want)).astype(BF16)


def _in_proj_kernel(x_ref, pos_ref, freq_ref, expand_ref, g_ref, w32_ref,
                    u_ref, q_ref, k_ref, v_ref, w_ref):
    @pl.when(pl.program_id(0) == 0)
    def _():
        w_ref[...] = w32_ref[...].astype(BF16)

    rows = x_ref.shape[0]
    piece = IN_PIECE
    assert rows % piece == 0

    ang = pos_ref[0].astype(F32) * freq_ref[...]
    small = jnp.concatenate([jnp.cos(ang), jnp.sin(ang)], axis=0)
    hi = small.astype(BF16).astype(F32)
    mid = (small - hi).astype(BF16).astype(F32)
    lo = (small - hi - mid).astype(BF16).astype(F32)
    pad = jnp.zeros((LANES - 3 * small.shape[0], small.shape[1]), F32)
    pieces = jnp.concatenate([hi, mid, lo, pad], axis=0).T.astype(BF16)
    table = _dot(pieces, expand_ref[...])
    d = lax.broadcasted_iota(jnp.int32, (1, LANES), 1) & (HEAD_DIM - 1)
    cos = table[:, :LANES] + jnp.where(d < ROPE_DIM, 0.0, 1.0)
    sin = table[:, LANES:]
    sin_from_hi = jnp.where(d < ROPE_HALF, -sin, 0.0)
    sin_from_lo = jnp.where(d >= ROPE_HALF, sin, 0.0)

    q0 = 2 * CONV_CH
    k0 = q0 + ATTN_WIDTH
    v0 = k0 + ATTN_WIDTH
    scale = HEAD_DIM ** -0.5 * LOG2E

    def epilogue(r, z_u, z_qkv):
        rs = slice(r * piece, (r + 1) * piece)
        u = z_u[:, :CONV_CH] * jax.nn.sigmoid(z_u[:, CONV_CH:])
        u_ref[rs, :] = u
        made = [u]

        def rope(t):
            return (t * cos[rs]
                    + pltpu.roll(t, ROPE_HALF, 1) * sin_from_lo[rs]
                    + pltpu.roll(t, LANES - ROPE_HALF, 1) * sin_from_hi[rs])

        for c in range(ATTN_WIDTH // LANES):
            sl = slice(c * LANES, (c + 1) * LANES)
            qc = rope(z_qkv[:, c * LANES:(c + 1) * LANES]) * scale
            kc = rope(z_qkv[:, ATTN_WIDTH + c * LANES:ATTN_WIDTH + (c + 1) * LANES])
            q_ref[rs, sl] = qc.astype(BF16)
            k_ref[rs, sl] = kc.astype(BF16)
            made += [qc, kc]
        v_ref[rs, :] = z_qkv[:, 2 * ATTN_WIDTH:].astype(BF16)
        return made

    norm = lambda r: _rmsnorm(x_ref[r * piece:(r + 1) * piece, :], g_ref[...]).astype(BF16)
    prev = None
    for r in range(rows // piece):
        hn = norm(r)
        z_u = _dot(hn, w_ref[:, :q0])
        if prev is not None:
            made = epilogue(r - 1, *prev)
            parts = [m[a:a + SUBLANES, c:c + LANES] for m in made
                     for a in range(0, piece, SUBLANES) for c in range(0, m.shape[1], LANES)]
            while len(parts) > 1:
                pairs = [jnp.maximum(parts[n], parts[n + 1]) for n in range(0, len(parts) - 1, 2)]
                parts = pairs + parts[len(parts) - len(parts) % 2:]
            zero = parts[0] - parts[0]
            head = hn[:BF16_ROWS, :LANES] + jnp.concatenate([zero, zero], axis=0).astype(BF16)
            top = jnp.concatenate([head, hn[:BF16_ROWS, LANES:]], axis=1)
            hn = jnp.concatenate([top, hn[BF16_ROWS:]], axis=0)
        prev = (z_u, _dot(hn, w_ref[:, q0:]))
    epilogue(rows // piece - 1, *prev)


def _in_proj(x2, positions, inv_freq, g, w):
    t = x2.shape[0]
    rows = IN_ROWS
    pos3 = positions.reshape(t // rows, 1, rows)
    freq = jnp.broadcast_to(inv_freq[:, None], (ROPE_HALF, rows))
    expand = _rope_expand_matrix()
    row_spec = lambda width: pl.BlockSpec((rows, width), lambda i: (i, 0))
    full = lambda a: pl.BlockSpec(a.shape, lambda i: (0, 0))
    return pl.pallas_call(
        _in_proj_kernel,
        grid=(t // rows,),
        in_specs=[row_spec(D_MODEL), pl.BlockSpec((1, 1, rows), lambda i: (i, 0, 0)),
                  full(freq), full(expand), full(g),
                  pl.BlockSpec(w.shape, lambda i: (0, 0), pipeline_mode=pl.Buffered(1))],
        out_specs=[row_spec(CONV_CH), row_spec(ATTN_WIDTH), row_spec(ATTN_WIDTH), row_spec(ATTN_WIDTH)],
        out_shape=[jax.ShapeDtypeStruct((t, CONV_CH), F32),
                   jax.ShapeDtypeStruct((t, ATTN_WIDTH), BF16),
                   jax.ShapeDtypeStruct((t, ATTN_WIDTH), BF16),
                   jax.ShapeDtypeStruct((t, ATTN_WIDTH), BF16)],
        scratch_shapes=[pltpu.VMEM(w.shape, BF16)],
        compiler_params=pltpu.CompilerParams(
            dimension_semantics=("arbitrary",), vmem_limit_bytes=VMEM_LIMIT),
        name="in_proj",
    )(x2, pos3, freq, expand, g, w)


def _conv_rows(window, w_ref, b_ref, lg_ref, lb_ref, not_before):
    first = CONV_PAD - (CONV_KERNEL - 1)
    win_rows = CONV_ROWS + CONV_PAD
    a = not_before[:BF16_ROWS, :LANES].astype(F32)[0:1, :]
    hold = a - a
    accs = []
    for g in range(CONV_CH // LANES):
        lanes = slice(g * LANES, (g + 1) * LANES)
        win = window(lanes)
        acc = jnp.broadcast_to(b_ref[:, lanes] + hold, (CONV_ROWS, LANES))
        for phase in range(SUBLANES):
            shifted = win if phase == 0 else pltpu.roll(win, win_rows - phase, 0)
            for base in range(0, CONV_PAD + 1, SUBLANES):
                j = base + phase - first
                if 0 <= j < CONV_KERNEL:
                    acc = acc + w_ref[j:j + 1, lanes] * shifted[base:base + CONV_ROWS]
        accs.append(acc)
        hold = acc[0:1, :] - acc[0:1, :]
    acc = jnp.concatenate(accs, axis=1)
    mu = jnp.mean(acc, axis=-1, keepdims=True)
    dev = acc - mu
    var = jnp.mean(dev * dev, axis=-1, keepdims=True)
    y = dev * lax.rsqrt(var + EPS) * lg_ref[...] + lb_ref[...]
    return (y * jax.nn.sigmoid(y)).astype(BF16)


def _attn_kernel(n_weights, q_ref, k_ref, v_ref, *refs):
    w32_refs = refs[:n_weights]
    o_ref = refs[n_weights]
    w16_refs = refs[n_weights + 1:]
    for src, dst in zip(w32_refs, w16_refs):
        dst[...] = src[...].astype(BF16)

    seq = q_ref.shape[0]
    nb = seq // MOBA_BLOCK
    lane = lax.broadcasted_iota(jnp.int32, (1, LANES), 1)
    heads = HEADS_PER_STEP
    cols = heads * MOBA_BLOCK
    key_row = lax.broadcasted_iota(jnp.int32, (MOBA_BLOCK, cols), 0)
    q_col = lax.broadcasted_iota(jnp.int32, (MOBA_BLOCK, cols), 1) & (MOBA_BLOCK - 1)
    causal = key_row <= q_col
    blk = lax.broadcasted_iota(jnp.int32, (nb, MOBA_BLOCK), 0)
    first_ranked = MOBA_TOPK + 1
    in_head = [(lane >= h * HEAD_DIM) & (lane < (h + 1) * HEAD_DIM) for h in range(heads)]

    pairs = []
    for pr in range(q_ref.shape[1] // LANES):
        ls = slice(pr * LANES, (pr + 1) * LANES)
        q = q_ref[:, ls]
        k = k_ref[:, ls]
        vt = v_ref[:, ls].astype(F32).T.astype(BF16)
        kmean = jnp.mean(k.astype(F32).reshape(nb, MOBA_BLOCK, LANES), axis=1)
        vt_aug = jnp.concatenate([vt, jnp.ones((BF16_ROWS, seq), BF16)], axis=0)
        gate_t = [_dot_nt(jnp.where(m, kmean, 0.0).astype(BF16), q[first_ranked * MOBA_BLOCK:])
                  for m in in_head]
        pairs.append((q, k, vt_aug, gate_t))

    def scores(pr, i):
        q, k = pairs[pr][:2]
        qi = q[i * MOBA_BLOCK:(i + 1) * MOBA_BLOCK]
        q2 = jnp.concatenate([jnp.where(m, qi, jnp.zeros_like(qi)) for m in in_head], axis=0)
        return _dot_nt(k[:(i + 1) * MOBA_BLOCK], q2)

    def finish(pr, i, st):
        vt_aug, gate_t = pairs[pr][2:]
        if i >= first_ranked:
            caps = []
            for h in range(heads):
                c0 = (i - first_ranked) * MOBA_BLOCK
                gi = gate_t[h][:, c0:c0 + MOBA_BLOCK]
                rank = jnp.zeros((nb, MOBA_BLOCK), jnp.int32)
                for m in range(i):
                    gm = gi[m:m + 1, :]
                    beats = (gm > gi) | ((gm == gi) & (blk > m))
                    rank = rank + jnp.where(beats, 1, 0)
                caps.append(jnp.where(rank < MOBA_TOPK, POS, NEG))
            cap = jnp.concatenate(caps, axis=1)
        pieces = []
        for j in range(i + 1):
            sj = st[j * MOBA_BLOCK:(j + 1) * MOBA_BLOCK]
            if j == i:
                sj = jnp.where(causal, sj, NEG)
            elif i >= first_ranked:
                sj = jnp.minimum(sj, cap[j:j + 1, :])
            pieces.append(sj)
        st = pieces[0] if i == 0 else jnp.concatenate(pieces, axis=0)
        m_col = jnp.max(st, axis=0, keepdims=True)
        p = jnp.exp2(st - m_col).astype(BF16)
        ot = _dot(vt_aug[:, :(i + 1) * MOBA_BLOCK], p)
        denom = ot[LANES:LANES + 1]
        out_t = jnp.concatenate(
            [ot[h * HEAD_DIM:(h + 1) * HEAD_DIM, h * MOBA_BLOCK:(h + 1) * MOBA_BLOCK]
             / denom[:, h * MOBA_BLOCK:(h + 1) * MOBA_BLOCK] for h in range(heads)], axis=0)
        o_ref[i * MOBA_BLOCK:(i + 1) * MOBA_BLOCK, pr * LANES:(pr + 1) * LANES] = (
            out_t.T.astype(BF16))

    units = [(pr, i) for pr in range(len(pairs)) for i in range(nb)]
    st_next = scores(*units[0])
    for n, unit in enumerate(units):
        st = st_next
        if n + 1 < len(units):
            st_next = scores(*units[n + 1])
        finish(*unit, st)


def _attn(q2, k2, v2, bsz, seq, weights):
    width = ATTN_PAIRS * LANES
    pairs = ATTN_WIDTH // width
    n_steps = bsz * pairs
    spec = pl.BlockSpec((seq, width), lambda b, hp: (b, hp))

    def slice_spec(w):
        n = next(n for n in (n_steps, n_steps // 2, n_steps // 4, 1)
                 if w.shape[0] % n == 0 and (w.shape[0] // n) % BF16_ROWS == 0)
        return pl.BlockSpec((w.shape[0] // n, w.shape[1]),
                            lambda b, hp: (jnp.minimum(b * pairs + hp, n - 1), 0))

    w_specs = [slice_spec(w) for w in weights]
    out = pl.pallas_call(
        functools.partial(_attn_kernel, len(weights)),
        grid=(bsz, pairs),
        in_specs=[spec, spec, spec] + w_specs,
        out_specs=[spec] + w_specs,
        out_shape=[jax.ShapeDtypeStruct(q2.shape, BF16)]
        + [jax.ShapeDtypeStruct(w.shape, BF16) for w in weights],
        compiler_params=pltpu.CompilerParams(
            dimension_semantics=("arbitrary", "arbitrary"), vmem_limit_bytes=VMEM_LIMIT),
        name="attn",
    )(q2, k2, v2, *weights)
    return out[0], out[1:]


def _ordered_after(x, anchor):
    rows, width = anchor.shape
    parts = [anchor[r:r + BF16_ROWS, c:c + LANES]
             for r in range(0, rows, BF16_ROWS) for c in range(0, width, LANES)]
    while len(parts) > 1:
        parts = [jnp.maximum(parts[n], parts[n + 1]) for n in range(0, len(parts), 2)]
    a = parts[0]
    head = x[:BF16_ROWS, :LANES] + (a - a)
    top = jnp.concatenate([head, x[:BF16_ROWS, LANES:]], axis=1)
    return jnp.concatenate([top, x[BF16_ROWS:]], axis=0)


def _tail_kernel(tiles_per_seq, u0_ref, u_ref, halo_ref, x_ref, a_ref, p_ref, cw_ref, cb_ref, lg_ref,
                 lb_ref, wout_ref, gf_ref, wup_ref, wdn_ref, gp_ref, wg_ref, wp_ref, gfin_ref,
                 o_ref, conv_ref):
    i = pl.program_id(0)
    n_tiles = pl.num_programs(0)

    def conv_chunk_of(tile_ref, halo, c, not_before):
        def window(lanes):
            start = c * CONV_ROWS - CONV_PAD
            stop = (c + 1) * CONV_ROWS
            if start < 0:
                return jnp.concatenate([halo[CONV_PAD + start:, lanes], tile_ref[0:stop, lanes]],
                                       axis=0)
            return tile_ref[start:stop, lanes]
        y = _conv_rows(window, cw_ref, cb_ref, lg_ref, lb_ref, not_before)
        conv_ref[c * CONV_ROWS:(c + 1) * CONV_ROWS, :] = y
        return y

    @pl.when(i == 0)
    def _():
        zeros = jnp.zeros((CONV_PAD, CONV_CH), F32)
        start = u0_ref[0:BF16_ROWS, 0:LANES].astype(BF16)
        for c in range(u0_ref.shape[0] // CONV_ROWS):
            conv_chunk_of(u0_ref, zeros, c, start)

    tile = jnp.minimum(i + 1, n_tiles - 1)
    starts_seq = lax.rem(tile, tiles_per_seq) == 0
    halo = jnp.where(starts_seq, 0.0, halo_ref[...])
    conv_chunk = functools.partial(conv_chunk_of, u_ref, halo)

    h = (x_ref[...]
         + _dot(conv_ref[...], wout_ref[:CONV_CH, :])
         + _dot(a_ref[...], wout_ref[CONV_CH:, :]))
    hn = _rmsnorm(h, gf_ref[...]).astype(BF16)
    n_chunks = u_ref.shape[0] // CONV_ROWS
    bounds = [0, (4 * n_chunks) // 8, (7 * n_chunks) // 8, n_chunks]

    def conv_group(q, x, not_before):
        for c in range(bounds[q], bounds[q + 1]):
            y = conv_chunk(c, not_before)
            x = _ordered_after(x, y)
            not_before = y
        return x

    gt = _dot(hn, wup_ref[:, :D_FF])
    hn = conv_group(0, hn, hn)
    up = _dot(hn, wup_ref[:, D_FF:])
    act = conv_group(1, (jax.nn.silu(gt) * up).astype(BF16), hn)
    h = h + _dot(act, wdn_ref[...])
    hn = conv_group(2, _rmsnorm(h, gp_ref[...]).astype(BF16), act)
    gate = jax.nn.sigmoid(_dot(hn, wg_ref[...]))
    h = h + gate * _dot(p_ref[...].astype(BF16), wp_ref[...])
    o_ref[...] = _rmsnorm(h, gfin_ref[...])


def _tail(u2, x2, attn2, p2, seq, cw, cb, lg, lb, wout, gf, wup, wdn, gp, wg, wp, gfin):
    t = x2.shape[0]
    rows = TAIL_ROWS
    n_tiles = t // rows
    halo_per_tile = rows // CONV_PAD
    nxt = lambda i: jnp.minimum(i + 1, n_tiles - 1)
    row_spec = lambda width: pl.BlockSpec((rows, width), lambda i: (i, 0))
    full = lambda a: pl.BlockSpec(a.shape, lambda i: (0, 0), pipeline_mode=pl.Buffered(1))
    return pl.pallas_call(
        functools.partial(_tail_kernel, seq // rows),
        grid=(n_tiles,),
        in_specs=[pl.BlockSpec((rows, CONV_CH), lambda i: (0, 0)),
                  pl.BlockSpec((rows, CONV_CH), lambda i: (nxt(i), 0)),
                  pl.BlockSpec((CONV_PAD, CONV_CH), lambda i: (nxt(i) * halo_per_tile - 1, 0)),
                  row_spec(D_MODEL), row_spec(ATTN_WIDTH), row_spec(PLE_DIM),
                  full(cw), full(cb), full(lg), full(lb),
                  full(wout), full(gf), full(wup), full(wdn), full(gp), full(wg), full(wp),
                  full(gfin)],
        out_specs=row_spec(D_MODEL),
        out_shape=jax.ShapeDtypeStruct((t, D_MODEL), F32),
        scratch_shapes=[pltpu.VMEM((rows, CONV_CH), BF16)],
        compiler_params=pltpu.CompilerParams(
            dimension_semantics=("arbitrary",), vmem_limit_bytes=VMEM_LIMIT),
        name="tail",
    )(u2, u2, u2, x2, attn2, p2, cw, cb, lg, lb, wout, gf, wup, wdn, gp, wg, wp, gfin)


def kernel(x, p, positions, norm_mix_g, w_in, conv_w, conv_b, conv_ln_g, conv_ln_b, w_out,
           norm_ffn_g, w_ffn_up, w_ffn_down, norm_ple_g, w_ple_gate, w_ple_proj, final_norm_g):
    bsz, seq, _ = x.shape
    t = bsz * seq
    depth = w_in.shape[0]
    row = lambda a: a.reshape(1, -1)
    inv_freq = ROPE_THETA ** (-jnp.arange(0, ROPE_DIM, 2, dtype=F32) / ROPE_DIM)
    h = x.reshape(t, D_MODEL)
    for i in range(depth):
        u, q, k, v = _in_proj(h, positions, inv_freq, row(norm_mix_g[i]), w_in[i])
        attn, (wout, wup, wdn, wg, wp) = _attn(
            q, k, v, bsz, seq,
            [w_out[i], w_ffn_up[i], w_ffn_down[i], w_ple_gate[i], w_ple_proj[i]])
        assert depth == 1
        h = _tail(u, h, attn, p[i].reshape(t, PLE_DIM), seq, conv_w[i], row(conv_b[i]),
                  row(conv_ln_g[i]), row(conv_ln_b[i]), wout,
                  row(norm_ffn_g[i]), wup, wdn, row(norm_ple_g[i]), wg, wp,
                  row(final_norm_g))
    return h.reshape(bsz, seq, D_MODEL)
```

```python
import functools

import jax
import jax.numpy as jnp
from jax import lax
from jax.experimental import pallas as pl
from jax.experimental.pallas import tpu as pltpu

D_MODEL = 1024
CONV_CH = 512
ATTN_HEADS = 8
HEAD_DIM = 64
ATTN_WIDTH = ATTN_HEADS * HEAD_DIM
IN_WIDTH = 2 * CONV_CH + 3 * ATTN_WIDTH
CONV_KERNEL = 31
MOBA_BLOCK = 256
MOBA_TOPK = 3
ROPE_THETA = 500000.0
ROPE_DIM = HEAD_DIM // 4
ROPE_HALF = ROPE_DIM // 2
D_FF = -(-8 * D_MODEL // (3 * 256)) * 256
PLE_DIM = 256
EPS = 1e-6

LANES = 128
SUBLANES = 8
BF16_ROWS = 16
LOG2E = 1.4426950408889634
HEADS_PER_STEP = LANES // HEAD_DIM
CONV_PAD = 32
CONV_ROWS = 32
IN_ROWS = 1024
IN_PIECE = 256
TAIL_ROWS = 512
ATTN_PAIRS = 2
VMEM_LIMIT = 56 * 1024 * 1024

F32 = jnp.float32
BF16 = jnp.bfloat16
NEG = float(jnp.finfo(jnp.float32).min)
POS = float(jnp.finfo(jnp.float32).max)


def _rmsnorm(x, g):
    return x * lax.rsqrt(jnp.mean(x * x, axis=-1, keepdims=True) + EPS) * g


def _dot(a, b):
    return jnp.dot(a, b, preferred_element_type=F32)


def _dot_nt(a, b):
    return lax.dot_general(a, b, (((1,), (1,)), ((), ())), preferred_element_type=F32)


def _rope_expand_matrix():
    r = jnp.arange(LANES)[:, None]
    l = jnp.arange(2 * LANES)[None, :]
    k = r % (2 * ROPE_HALF)
    piece_ok = r < 3 * 2 * ROPE_HALF
    d = (l % LANES) % HEAD_DIM
    want = d % ROPE_HALF + ROPE_HALF * (l // LANES)
    return (piece_ok & (d < ROPE_DIM) & (k == want)).astype(BF16)


def _in_proj_kernel(x_ref, pos_ref, freq_ref, expand_ref, g_ref, w32_ref,
                    u_ref, q_ref, k_ref, v_ref, w_ref):
    @pl.when(pl.program_id(0) == 0)
    def _():
        w_ref[...] = w32_ref[...].astype(BF16)

    rows = x_ref.shape[0]
    piece = IN_PIECE
    assert rows % piece == 0

    ang = pos_ref[0].astype(F32) * freq_ref[...]
    small = jnp.concatenate([jnp.cos(ang), jnp.sin(ang)], axis=0)
    hi = small.astype(BF16).astype(F32)
    mid = (small - hi).astype(BF16).astype(F32)
    lo = (small - hi - mid).astype(BF16).astype(F32)
    pad = jnp.zeros((LANES - 3 * small.shape[0], small.shape[1]), F32)
    pieces = jnp.concatenate([hi, mid, lo, pad], axis=0).T.astype(BF16)
    table = _dot(pieces, expand_ref[...])
    d = lax.broadcasted_iota(jnp.int32, (1, LANES), 1) & (HEAD_DIM - 1)
    cos = table[:, :LANES] + jnp.where(d < ROPE_DIM, 0.0, 1.0)
    sin = table[:, LANES:]
    sin_from_hi = jnp.where(d < ROPE_HALF, -sin, 0.0)
    sin_from_lo = jnp.where(d >= ROPE_HALF, sin, 0.0)

    q0 = 2 * CONV_CH
    k0 = q0 + ATTN_WIDTH
    v0 = k0 + ATTN_WIDTH
    scale = HEAD_DIM ** -0.5 * LOG2E

    def epilogue(r, z_u, z_qkv):
        rs = slice(r * piece, (r + 1) * piece)
        u = z_u[:, :CONV_CH] * jax.nn.sigmoid(z_u[:, CONV_CH:])
        u_ref[rs, :] = u
        made = [u]

        def rope(t):
            return (t * cos[rs]
                    + pltpu.roll(t, ROPE_HALF, 1) * sin_from_lo[rs]
                    + pltpu.roll(t, LANES - ROPE_HALF, 1) * sin_from_hi[rs])

        for c in range(ATTN_WIDTH // LANES):
            sl = slice(c * LANES, (c + 1) * LANES)
            qc = rope(z_qkv[:, c * LANES:(c + 1) * LANES]) * scale
            kc = rope(z_qkv[:, ATTN_WIDTH + c * LANES:ATTN_WIDTH + (c + 1) * LANES])
            q_ref[rs, sl] = qc.astype(BF16)
            k_ref[rs, sl] = kc.astype(BF16)
            made += [qc, kc]
        v_ref[rs, :] = z_qkv[:, 2 * ATTN_WIDTH:].astype(BF16)
        return made

    norm = lambda r: _rmsnorm(x_ref[r * piece:(r + 1) * piece, :], g_ref[...]).astype(BF16)
    prev = None
    for r in range(rows // piece):
        hn = norm(r)
        z_u = _dot(hn, w_ref[:, :q0])
        if prev is not None:
            made = epilogue(r - 1, *prev)
            parts = [m[a:a + SUBLANES, c:c + LANES] for m in made
                     for a in range(0, piece, SUBLANES) for c in range(0, m.shape[1], LANES)]
            while len(parts) > 1:
                pairs = [jnp.maximum(parts[n], parts[n + 1]) for n in range(0, len(parts) - 1, 2)]
                parts = pairs + parts[len(parts) - len(parts) % 2:]
            zero = parts[0] - parts[0]
            head = hn[:BF16_ROWS, :LANES] + jnp.concatenate([zero, zero], axis=0).astype(BF16)
            top = jnp.concatenate([head, hn[:BF16_ROWS, LANES:]], axis=1)
            hn = jnp.concatenate([top, hn[BF16_ROWS:]], axis=0)
        prev = (z_u, _dot(hn, w_ref[:, q0:]))
    epilogue(rows // piece - 1, *prev)


def _in_proj(x2, positions, inv_freq, g, w):
    t = x2.shape[0]
    rows = IN_ROWS
    pos3 = positions.reshape(t // rows, 1, rows)
    freq = jnp.broadcast_to(inv_freq[:, None], (ROPE_HALF, rows))
    expand = _rope_expand_matrix()
    row_spec = lambda width: pl.BlockSpec((rows, width), lambda i: (i, 0))
    full = lambda a: pl.BlockSpec(a.shape, lambda i: (0, 0))
    return pl.pallas_call(
        _in_proj_kernel,
        grid=(t // rows,),
        in_specs=[row_spec(D_MODEL), pl.BlockSpec((1, 1, rows), lambda i: (i, 0, 0)),
                  full(freq), full(expand), full(g),
                  pl.BlockSpec(w.shape, lambda i: (0, 0), pipeline_mode=pl.Buffered(1))],
        out_specs=[row_spec(CONV_CH), row_spec(ATTN_WIDTH), row_spec(ATTN_WIDTH), row_spec(ATTN_WIDTH)],
        out_shape=[jax.ShapeDtypeStruct((t, CONV_CH), F32),
                   jax.ShapeDtypeStruct((t, ATTN_WIDTH), BF16),
                   jax.ShapeDtypeStruct((t, ATTN_WIDTH), BF16),
                   jax.ShapeDtypeStruct((t, ATTN_WIDTH), BF16)],
        scratch_shapes=[pltpu.VMEM(w.shape, BF16)],
        compiler_params=pltpu.CompilerParams(
            dimension_semantics=("arbitrary",), vmem_limit_bytes=VMEM_LIMIT),
        name="in_proj",
    )(x2, pos3, freq, expand, g, w)


def _conv_rows(window, w_ref, b_ref, lg_ref, lb_ref, not_before):
    first = CONV_PAD - (CONV_KERNEL - 1)
    win_rows = CONV_ROWS + CONV_PAD
    a = not_before[:BF16_ROWS, :LANES].astype(F32)[0:1, :]
    hold = a - a
    accs = []
    for g in range(CONV_CH // LANES):
        lanes = slice(g * LANES, (g + 1) * LANES)
        win = window(lanes)
        acc = jnp.broadcast_to(b_ref[:, lanes] + hold, (CONV_ROWS, LANES))
        for phase in range(SUBLANES):
            shifted = win if phase == 0 else pltpu.roll(win, win_rows - phase, 0)
            for base in range(0, CONV_PAD + 1, SUBLANES):
                j = base + phase - first
                if 0 <= j < CONV_KERNEL:
                    acc = acc + w_ref[j:j + 1, lanes] * shifted[base:base + CONV_ROWS]
        accs.append(acc)
        hold = acc[0:1, :] - acc[0:1, :]
    acc = jnp.concatenate(accs, axis=1)
    mu = jnp.mean(acc, axis=-1, keepdims=True)
    dev = acc - mu
    var = jnp.mean(dev * dev, axis=-1, keepdims=True)
    y = dev * lax.rsqrt(var + EPS) * lg_ref[...] + lb_ref[...]
    return (y * jax.nn.sigmoid(y)).astype(BF16)


def _attn_kernel(n_weights, q_ref, k_ref, v_ref, *refs):
    w32_refs = refs[:n_weights]
    o_ref = refs[n_weights]
    w16_refs = refs[n_weights + 1:]
    for src, dst in zip(w32_refs, w16_refs):
        dst[...] = src[...].astype(BF16)

    seq = q_ref.shape[0]
    nb = seq // MOBA_BLOCK
    lane = lax.broadcasted_iota(jnp.int32, (1, LANES), 1)
    heads = HEADS_PER_STEP
    cols = heads * MOBA_BLOCK
    key_row = lax.broadcasted_iota(jnp.int32, (MOBA_BLOCK, cols), 0)
    q_col = lax.broadcasted_iota(jnp.int32, (MOBA_BLOCK, cols), 1) & (MOBA_BLOCK - 1)
    causal = key_row <= q_col
    blk = lax.broadcasted_iota(jnp.int32, (nb, MOBA_BLOCK), 0)
    first_ranked = MOBA_TOPK + 1
    in_head = [(lane >= h * HEAD_DIM) & (lane < (h + 1) * HEAD_DIM) for h in range(heads)]

    pairs = []
    for pr in range(q_ref.shape[1] // LANES):
        ls = slice(pr * LANES, (pr + 1) * LANES)
        q = q_ref[:, ls]
        k = k_ref[:, ls]
        vt = v_ref[:, ls].astype(F32).T.astype(BF16)
        kmean = jnp.mean(k.astype(F32).reshape(nb, MOBA_BLOCK, LANES), axis=1)
        vt_aug = jnp.concatenate([vt, jnp.ones((BF16_ROWS, seq), BF16)], axis=0)
        gate_t = [_dot_nt(jnp.where(m, kmean, 0.0).astype(BF16), q[first_ranked * MOBA_BLOCK:])
                  for m in in_head]
        pairs.append((q, k, vt_aug, gate_t))

    def scores(pr, i):
        q, k = pairs[pr][:2]
        qi = q[i * MOBA_BLOCK:(i + 1) * MOBA_BLOCK]
        q2 = jnp.concatenate([jnp.where(m, qi, jnp.zeros_like(qi)) for m in in_head], axis=0)
        return _dot_nt(k[:(i + 1) * MOBA_BLOCK], q2)

    def finish(pr, i, st):
        vt_aug, gate_t = pairs[pr][2:]
        if i >= first_ranked:
            caps = []
            for h in range(heads):
                c0 = (i - first_ranked) * MOBA_BLOCK
                gi = gate_t[h][:, c0:c0 + MOBA_BLOCK]
                rank = jnp.zeros((nb, MOBA_BLOCK), jnp.int32)
                for m in range(i):
                    gm = gi[m:m + 1, :]
                    beats = (gm > gi) | ((gm == gi) & (blk > m))
                    rank = rank + jnp.where(beats, 1, 0)
                caps.append(jnp.where(rank < MOBA_TOPK, POS, NEG))
            cap = jnp.concatenate(caps, axis=1)
        pieces = []
        for j in range(i):
            sj = st[j * MOBA_BLOCK:(j + 1) * MOBA_BLOCK]
            if i >= first_ranked:
                sj = jnp.minimum(sj, cap[j:j + 1, :])
            pieces.append(sj)
        half = MOBA_BLOCK // 2
        diag = st[i * MOBA_BLOCK:(i + 1) * MOBA_BLOCK]
        kr = lax.broadcasted_iota(jnp.int32, (half, cols), 0)
        qc = lax.broadcasted_iota(jnp.int32, (half, cols), 1) & (MOBA_BLOCK - 1)
        pieces.append(jnp.where(kr <= qc, diag[:half], NEG))

        def late(a):
            return jnp.concatenate(
                [a[:, h * MOBA_BLOCK + half:(h + 1) * MOBA_BLOCK] for h in range(heads)], axis=1)

        def spread(a, fill):
            gap = jnp.full((a.shape[0], half), fill, a.dtype)
            return jnp.concatenate(
                [x for h in range(heads) for x in (gap, a[:, h * half:(h + 1) * half])], axis=1)

        kr2 = lax.broadcasted_iota(jnp.int32, (half, heads * half), 0)
        qc2 = lax.broadcasted_iota(jnp.int32, (half, heads * half), 1) & (half - 1)
        bot = jnp.where(kr2 <= qc2, late(diag[half:]), NEG)
        top = pieces[0] if len(pieces) == 1 else jnp.concatenate(pieces, axis=0)
        bot_full = spread(bot, NEG)
        m_col = jnp.maximum(jnp.max(top, axis=0, keepdims=True),
                            jnp.max(bot_full, axis=0, keepdims=True))
        p = jnp.concatenate(
            [jnp.exp2(top - m_col).astype(BF16),
             spread(jnp.exp2(late(bot_full - m_col)), 0.0).astype(BF16)], axis=0)
        ot = _dot(vt_aug[:, :(i + 1) * MOBA_BLOCK], p)
        denom = ot[LANES:LANES + 1]
        out_t = jnp.concatenate(
            [ot[h * HEAD_DIM:(h + 1) * HEAD_DIM, h * MOBA_BLOCK:(h + 1) * MOBA_BLOCK]
             / denom[:, h * MOBA_BLOCK:(h + 1) * MOBA_BLOCK] for h in range(heads)], axis=0)
        o_ref[i * MOBA_BLOCK:(i + 1) * MOBA_BLOCK, pr * LANES:(pr + 1) * LANES] = (
            out_t.T.astype(BF16))

    units = [(pr, i) for pr in range(len(pairs)) for i in range(nb)]
    st_next = scores(*units[0])
    for n, unit in enumerate(units):
        st = st_next
        if n + 1 < len(units):
            st_next = scores(*units[n + 1])
        finish(*unit, st)


def _attn(q2, k2, v2, bsz, seq, weights):
    width = ATTN_PAIRS * LANES
    pairs = ATTN_WIDTH // width
    n_steps = bsz * pairs
    spec = pl.BlockSpec((seq, width), lambda b, hp: (b, hp))

    def slice_spec(w):
        n = next(n for n in (n_steps, n_steps // 2, n_steps // 4, 1)
                 if w.shape[0] % n == 0 and (w.shape[0] // n) % BF16_ROWS == 0)
        return pl.BlockSpec((w.shape[0] // n, w.shape[1]),
                            lambda b, hp: (jnp.minimum(b * pairs + hp, n - 1), 0))

    w_specs = [slice_spec(w) for w in weights]
    out = pl.pallas_call(
        functools.partial(_attn_kernel, len(weights)),
        grid=(bsz, pairs),
        in_specs=[spec, spec, spec] + w_specs,
        out_specs=[spec] + w_specs,
        out_shape=[jax.ShapeDtypeStruct(q2.shape, BF16)]
        + [jax.ShapeDtypeStruct(w.shape, BF16) for w in weights],
        compiler_params=pltpu.CompilerParams(
            dimension_semantics=("arbitrary", "arbitrary"), vmem_limit_bytes=VMEM_LIMIT),
        name="attn",
    )(q2, k2, v2, *weights)
    return out[0], out[1:]


def _ordered_after(x, anchor):
    rows, width = anchor.shape
    parts = [anchor[r:r + BF16_ROWS, c:c + LANES]
             for r in range(0, rows, BF16_ROWS) for c in range(0, width, LANES)]
    while len(parts) > 1:
        parts = [jnp.maximum(parts[n], parts[n + 1]) for n in range(0, len(parts), 2)]
    a = parts[0]
    head = x[:BF16_ROWS, :LANES] + (a - a)
    top = jnp.concatenate([head, x[:BF16_ROWS, LANES:]], axis=1)
    return jnp.concatenate([top, x[BF16_ROWS:]], axis=0)


def _tail_kernel(tiles_per_seq, u0_ref, u_ref, halo_ref, x_ref, a_ref, p_ref, cw_ref, cb_ref, lg_ref,
                 lb_ref, wout_ref, gf_ref, wup_ref, wdn_ref, gp_ref, wg_ref, wp_ref, gfin_ref,
                 o_ref, conv_ref):
    i = pl.program_id(0)
    n_tiles = pl.num_programs(0)

    def conv_chunk_of(tile_ref, halo, c, not_before):
        def window(lanes):
            start = c * CONV_ROWS - CONV_PAD
            stop = (c + 1) * CONV_ROWS
            if start < 0:
                return jnp.concatenate([halo[CONV_PAD + start:, lanes], tile_ref[0:stop, lanes]],
                                       axis=0)
            return tile_ref[start:stop, lanes]
        y = _conv_rows(window, cw_ref, cb_ref, lg_ref, lb_ref, not_before)
        conv_ref[c * CONV_ROWS:(c + 1) * CONV_ROWS, :] = y
        return y

    @pl.when(i == 0)
    def _():
        zeros = jnp.zeros((CONV_PAD, CONV_CH), F32)
        start = u0_ref[0:BF16_ROWS, 0:LANES].astype(BF16)
        for c in range(u0_ref.shape[0] // CONV_ROWS):
            conv_chunk_of(u0_ref, zeros, c, start)

    tile = jnp.minimum(i + 1, n_tiles - 1)
    starts_seq = lax.rem(tile, tiles_per_seq) == 0
    halo = jnp.where(starts_seq, 0.0, halo_ref[...])
    conv_chunk = functools.partial(conv_chunk_of, u_ref, halo)

    h = (x_ref[...]
         + _dot(conv_ref[...], wout_ref[:CONV_CH, :])
         + _dot(a_ref[...], wout_ref[CONV_CH:, :]))
    hn = _rmsnorm(h, gf_ref[...]).astype(BF16)
    n_chunks = u_ref.shape[0] // CONV_ROWS
    bounds = [0, (4 * n_chunks) // 8, (7 * n_chunks) // 8, n_chunks]

    def conv_group(q, x, not_before):
        for c in range(bounds[q], bounds[q + 1]):
            y = conv_chunk(c, not_before)
            x = _ordered_after(x, y)
            not_before = y
        return x

    gt = _dot(hn, wup_ref[:, :D_FF])
    hn = conv_group(0, hn, hn)
    up = _dot(hn, wup_ref[:, D_FF:])
    act = conv_group(1, (jax.nn.silu(gt) * up).astype(BF16), hn)
    h = h + _dot(act, wdn_ref[...])
    hn = conv_group(2, _rmsnorm(h, gp_ref[...]).astype(BF16), act)
    gate = jax.nn.sigmoid(_dot(hn, wg_ref[...]))
    h = h + gate * _dot(p_ref[...].astype(BF16), wp_ref[...])
    o_ref[...] = _rmsnorm(h, gfin_ref[...])


def _tail(u2, x2, attn2, p2, seq, cw, cb, lg, lb, wout, gf, wup, wdn, gp, wg, wp, gfin):
    t = x2.shape[0]
    rows = TAIL_ROWS
    n_tiles = t // rows
    halo_per_tile = rows // CONV_PAD
    nxt = lambda i: jnp.minimum(i + 1, n_tiles - 1)
    row_spec = lambda width: pl.BlockSpec((rows, width), lambda i: (i, 0))
    full = lambda a: pl.BlockSpec(a.shape, lambda i: (0, 0), pipeline_mode=pl.Buffered(1))
    return pl.pallas_call(
        functools.partial(_tail_kernel, seq // rows),
        grid=(n_tiles,),
        in_specs=[pl.BlockSpec((rows, CONV_CH), lambda i: (0, 0)),
                  pl.BlockSpec((rows, CONV_CH), lambda i: (nxt(i), 0)),
                  pl.BlockSpec((CONV_PAD, CONV_CH), lambda i: (nxt(i) * halo_per_tile - 1, 0)),
                  row_spec(D_MODEL), row_spec(ATTN_WIDTH), row_spec(PLE_DIM),
                  full(cw), full(cb), full(lg), full(lb),
                  full(wout), full(gf), full(wup), full(wdn), full(gp), full(wg), full(wp),
                  full(gfin)],
        out_specs=row_spec(D_MODEL),
        out_shape=jax.ShapeDtypeStruct((t, D_MODEL), F32),
        scratch_shapes=[pltpu.VMEM((rows, CONV_CH), BF16)],
        compiler_params=pltpu.CompilerParams(
            dimension_semantics=("arbitrary",), vmem_limit_bytes=VMEM_LIMIT),
        name="tail",
    )(u2, u2, u2, x2, attn2, p2, cw, cb, lg, lb, wout, gf, wup, wdn, gp, wg, wp, gfin)


def kernel(x, p, positions, norm_mix_g, w_in, conv_w, conv_b, conv_ln_g, conv_ln_b, w_out,
           norm_ffn_g, w_ffn_up, w_ffn_down, norm_ple_g, w_ple_gate, w_ple_proj, final_norm_g):
    bsz, seq, _ = x.shape
    t = bsz * seq
    depth = w_in.shape[0]
    row = lambda a: a.reshape(1, -1)
    inv_freq = ROPE_THETA ** (-jnp.arange(0, ROPE_DIM, 2, dtype=F32) / ROPE_DIM)
    h = x.reshape(t, D_MODEL)
    for i in range(depth):
        u, q, k, v = _in_proj(h, positions, inv_freq, row(norm_mix_g[i]), w_in[i])
        attn, (wout, wup, wdn, wg, wp) = _attn(
            q, k, v, bsz, seq,
            [w_out[i], w_ffn_up[i], w_ffn_down[i], w_ple_gate[i], w_ple_proj[i]])
        assert depth == 1
        h = _tail(u, h, attn, p[i].reshape(t, PLE_DIM), seq, conv_w[i], row(conv_b[i]),
                  row(conv_ln_g[i]), row(conv_ln_b[i]), wout,
                  row(norm_ffn_g[i]), wup, wdn, row(norm_ple_g[i]), wg, wp,
                  row(final_norm_g))
    return h.reshape(bsz, seq, D_MODEL)
```

```python
import functools

import jax
import jax.numpy as jnp
from jax import lax
from jax.experimental import pallas as pl
from jax.experimental.pallas import tpu as pltpu

D_MODEL = 1024
CONV_CH = 512
ATTN_HEADS = 8
HEAD_DIM = 64
ATTN_WIDTH = ATTN_HEADS * HEAD_DIM
IN_WIDTH = 2 * CONV_CH + 3 * ATTN_WIDTH
CONV_KERNEL = 31
MOBA_BLOCK = 256
MOBA_TOPK = 3
ROPE_THETA = 500000.0
ROPE_DIM = HEAD_DIM // 4
ROPE_HALF = ROPE_DIM // 2
D_FF = -(-8 * D_MODEL // (3 * 256)) * 256
PLE_DIM = 256
EPS = 1e-6

LANES = 128
SUBLANES = 8
BF16_ROWS = 16
LOG2E = 1.4426950408889634
HEADS_PER_STEP = LANES // HEAD_DIM
CONV_PAD = 32
CONV_ROWS = 32
IN_ROWS = 1024
IN_PIECE = 256
TAIL_ROWS = 512
ATTN_PAIRS = 2
VMEM_LIMIT = 56 * 1024 * 1024

F32 = jnp.float32
BF16 = jnp.bfloat16
NEG = float(jnp.finfo(jnp.float32).min)
POS = float(jnp.finfo(jnp.float32).max)


def _rmsnorm(x, g):
    return x * lax.rsqrt(jnp.mean(x * x, axis=-1, keepdims=True) + EPS) * g


def _dot(a, b):
    return jnp.dot(a, b, preferred_element_type=F32)


def _dot_nt(a, b):
    return lax.dot_general(a, b, (((1,), (1,)), ((), ())), preferred_element_type=F32)


def _rope_expand_matrix():
    r = jnp.arange(LANES)[:, None]
    l = jnp.arange(2 * LANES)[None, :]
    k = r % (2 * ROPE_HALF)
    piece_ok = r < 3 * 2 * ROPE_HALF
    d = (l % LANES) % HEAD_DIM
    want = d % ROPE_HALF + ROPE_HALF * (l // LANES)
    return (piece_ok & (d < ROPE_DIM) & (k == want)).astype(BF16)


def _in_proj_kernel(x_ref, pos_ref, freq_ref, expand_ref, g_ref, w32_ref,
                    u_ref, q_ref, k_ref, v_ref, w_ref):
    @pl.when(pl.program_id(0) == 0)
    def _():
        w_ref[...] = w32_ref[...].astype(BF16)

    rows = x_ref.shape[0]
    piece = IN_PIECE
    assert rows % piece == 0

    ang = pos_ref[0].astype(F32) * freq_ref[...]
    small = jnp.concatenate([jnp.cos(ang), jnp.sin(ang)], axis=0)
    hi = small.astype(BF16).astype(F32)
    mid = (small - hi).astype(BF16).astype(F32)
    lo = (small - hi - mid).astype(BF16).astype(F32)
    pad = jnp.zeros((LANES - 3 * small.shape[0], small.shape[1]), F32)
    pieces = jnp.concatenate([hi, mid, lo, pad], axis=0).T.astype(BF16)
    table = _dot(pieces, expand_ref[...])
    d = lax.broadcasted_iota(jnp.int32, (1, LANES), 1) & (HEAD_DIM - 1)
    cos = table[:, :LANES] + jnp.where(d < ROPE_DIM, 0.0, 1.0)
    sin = table[:, LANES:]
    sin_from_hi = jnp.where(d < ROPE_HALF, -sin, 0.0)
    sin_from_lo = jnp.where(d >= ROPE_HALF, sin, 0.0)

    q0 = 2 * CONV_CH
    k0 = q0 + ATTN_WIDTH
    v0 = k0 + ATTN_WIDTH
    scale = HEAD_DIM ** -0.5 * LOG2E

    def epilogue(r, z_u, z_qkv):
        rs = slice(r * piece, (r + 1) * piece)
        u = z_u[:, :CONV_CH] * jax.nn.sigmoid(z_u[:, CONV_CH:])
        u_ref[rs, :] = u
        made = [u]

        def rope(t):
            return (t * cos[rs]
                    + pltpu.roll(t, ROPE_HALF, 1) * sin_from_lo[rs]
                    + pltpu.roll(t, LANES - ROPE_HALF, 1) * sin_from_hi[rs])

        for c in range(ATTN_WIDTH // LANES):
            sl = slice(c * LANES, (c + 1) * LANES)
            qc = rope(z_qkv[:, c * LANES:(c + 1) * LANES]) * scale
            kc = rope(z_qkv[:, ATTN_WIDTH + c * LANES:ATTN_WIDTH + (c + 1) * LANES])
            q_ref[rs, sl] = qc.astype(BF16)
            k_ref[rs, sl] = kc.astype(BF16)
            made += [qc, kc]
        v_ref[rs, :] = z_qkv[:, 2 * ATTN_WIDTH:].astype(BF16)
        return made

    norm = lambda r: _rmsnorm(x_ref[r * piece:(r + 1) * piece, :], g_ref[...]).astype(BF16)
    prev = None
    for r in range(rows // piece):
        hn = norm(r)
        z_u = _dot(hn, w_ref[:, :q0])
        if prev is not None:
            made = epilogue(r - 1, *prev)
            parts = [m[a:a + SUBLANES, c:c + LANES] for m in made
                     for a in range(0, piece, SUBLANES) for c in range(0, m.shape[1], LANES)]
            while len(parts) > 1:
                pairs = [jnp.maximum(parts[n], parts[n + 1]) for n in range(0, len(parts) - 1, 2)]
                parts = pairs + parts[len(parts) - len(parts) % 2:]
            zero = parts[0] - parts[0]
            head = hn[:BF16_ROWS, :LANES] + jnp.concatenate([zero, zero], axis=0).astype(BF16)
            top = jnp.concatenate([head, hn[:BF16_ROWS, LANES:]], axis=1)
            hn = jnp.concatenate([top, hn[BF16_ROWS:]], axis=0)
        prev = (z_u, _dot(hn, w_ref[:, q0:]))
    epilogue(rows // piece - 1, *prev)


def _in_proj(x2, positions, inv_freq, g, w):
    t = x2.shape[0]
    rows = IN_ROWS
    pos3 = positions.reshape(t // rows, 1, rows)
    freq = jnp.broadcast_to(inv_freq[:, None], (ROPE_HALF, rows))
    expand = _rope_expand_matrix()
    row_spec = lambda width: pl.BlockSpec((rows, width), lambda i: (i, 0))
    full = lambda a: pl.BlockSpec(a.shape, lambda i: (0, 0))
    return pl.pallas_call(
        _in_proj_kernel,
        grid=(t // rows,),
        in_specs=[row_spec(D_MODEL), pl.BlockSpec((1, 1, rows), lambda i: (i, 0, 0)),
                  full(freq), full(expand), full(g),
                  pl.BlockSpec(w.shape, lambda i: (0, 0), pipeline_mode=pl.Buffered(1))],
        out_specs=[row_spec(CONV_CH), row_spec(ATTN_WIDTH), row_spec(ATTN_WIDTH), row_spec(ATTN_WIDTH)],
        out_shape=[jax.ShapeDtypeStruct((t, CONV_CH), F32),
                   jax.ShapeDtypeStruct((t, ATTN_WIDTH), BF16),
                   jax.ShapeDtypeStruct((t, ATTN_WIDTH), BF16),
                   jax.ShapeDtypeStruct((t, ATTN_WIDTH), BF16)],
        scratch_shapes=[pltpu.VMEM(w.shape, BF16)],
        compiler_params=pltpu.CompilerParams(
            dimension_semantics=("arbitrary",), vmem_limit_bytes=VMEM_LIMIT),
        name="in_proj",
    )(x2, pos3, freq, expand, g, w)


def _conv_rows(window, w_ref, b_ref, lg_ref, lb_ref, not_before):
    first = CONV_PAD - (CONV_KERNEL - 1)
    win_rows = CONV_ROWS + CONV_PAD
    a = not_before[:BF16_ROWS, :LANES].astype(F32)[0:1, :]
    hold = a - a
    accs = []
    for g in range(CONV_CH // LANES):
        lanes = slice(g * LANES, (g + 1) * LANES)
        win = window(lanes)
        acc = jnp.broadcast_to(b_ref[:, lanes] + hold, (CONV_ROWS, LANES))
        for phase in range(SUBLANES):
            shifted = win if phase == 0 else pltpu.roll(win, win_rows - phase, 0)
            for base in range(0, CONV_PAD + 1, SUBLANES):
                j = base + phase - first
                if 0 <= j < CONV_KERNEL:
                    acc = acc + w_ref[j:j + 1, lanes] * shifted[base:base + CONV_ROWS]
        accs.append(acc)
        hold = acc[0:1, :] - acc[0:1, :]
    acc = jnp.concatenate(accs, axis=1)
    mu = jnp.mean(acc, axis=-1, keepdims=True)
    dev = acc - mu
    var = jnp.mean(dev * dev, axis=-1, keepdims=True)
    y = dev * lax.rsqrt(var + EPS) * lg_ref[...] + lb_ref[...]
    return (y * jax.nn.sigmoid(y)).astype(BF16)


def _attn_kernel(n_weights, q_ref, k_ref, v_ref, *refs):
    w32_refs = refs[:n_weights]
    o_ref = refs[n_weights]
    w16_refs = refs[n_weights + 1:]
    for src, dst in zip(w32_refs, w16_refs):
        dst[...] = src[...].astype(BF16)

    seq = q_ref.shape[0]
    nb = seq // MOBA_BLOCK
    lane = lax.broadcasted_iota(jnp.int32, (1, LANES), 1)
    heads = HEADS_PER_STEP
    cols = heads * MOBA_BLOCK
    blk = lax.broadcasted_iota(jnp.int32, (nb, MOBA_BLOCK), 0)
    first_ranked = MOBA_TOPK + 1
    in_head = [(lane >= h * HEAD_DIM) & (lane < (h + 1) * HEAD_DIM) for h in range(heads)]

    pairs = []
    for pr in range(q_ref.shape[1] // LANES):
        ls = slice(pr * LANES, (pr + 1) * LANES)
        q = q_ref[:, ls]
        k = k_ref[:, ls]
        vt = v_ref[:, ls].astype(F32).T.astype(BF16)
        kmean = jnp.mean(k.astype(F32).reshape(nb, MOBA_BLOCK, LANES), axis=1)
        vt_aug = jnp.concatenate([vt, jnp.ones((BF16_ROWS, seq), BF16)], axis=0)
        gate_t = [_dot_nt(jnp.where(m, kmean, 0.0).astype(BF16), q[first_ranked * MOBA_BLOCK:])
                  for m in in_head]
        pairs.append((q, k, vt_aug, gate_t))

    def scores(pr, i):
        q, k = pairs[pr][:2]
        qi = q[i * MOBA_BLOCK:(i + 1) * MOBA_BLOCK]
        q2 = jnp.concatenate([jnp.where(m, qi, jnp.zeros_like(qi)) for m in in_head], axis=0)
        return _dot_nt(k[:(i + 1) * MOBA_BLOCK], q2)

    def finish(pr, i, st):
        vt_aug, gate_t = pairs[pr][2:]
        if i >= first_ranked:
            caps = []
            for h in range(heads):
                c0 = (i - first_ranked) * MOBA_BLOCK
                gi = gate_t[h][:, c0:c0 + MOBA_BLOCK]
                rank = jnp.zeros((nb, MOBA_BLOCK), jnp.int32)
                for m in range(i):
                    gm = gi[m:m + 1, :]
                    beats = (gm > gi) | ((gm == gi) & (blk > m))
                    rank = rank + jnp.where(beats, 1, 0)
                caps.append(jnp.where(rank < MOBA_TOPK, POS, NEG))
            cap = jnp.concatenate(caps, axis=1)
        pieces = []
        for j in range(i):
            sj = st[j * MOBA_BLOCK:(j + 1) * MOBA_BLOCK]
            if i >= first_ranked:
                sj = jnp.minimum(sj, cap[j:j + 1, :])
            pieces.append(sj)
        half = MOBA_BLOCK // 2
        diag = st[i * MOBA_BLOCK:(i + 1) * MOBA_BLOCK]
        kr = lax.broadcasted_iota(jnp.int32, (half, cols), 0)
        qc = lax.broadcasted_iota(jnp.int32, (half, cols), 1) & (MOBA_BLOCK - 1)
        pieces.append(jnp.where(kr <= qc, diag[:half], NEG))

        def late(a):
            return jnp.concatenate(
                [a[:, h * MOBA_BLOCK + half:(h + 1) * MOBA_BLOCK] for h in range(heads)], axis=1)

        def spread(a, fill):
            gap = jnp.full((a.shape[0], half), fill, a.dtype)
            return jnp.concatenate(
                [x for h in range(heads) for x in (gap, a[:, h * half:(h + 1) * half])], axis=1)

        kr2 = lax.broadcasted_iota(jnp.int32, (half, heads * half), 0)
        qc2 = lax.broadcasted_iota(jnp.int32, (half, heads * half), 1) & (half - 1)
        bot = jnp.where(kr2 <= qc2, late(diag[half:]), NEG)
        top = pieces[0] if len(pieces) == 1 else jnp.concatenate(pieces, axis=0)
        bot_full = spread(bot, NEG)
        m_col = jnp.maximum(jnp.max(top, axis=0, keepdims=True),
                            jnp.max(bot_full, axis=0, keepdims=True))
        p = jnp.concatenate(
            [jnp.exp2(top - m_col).astype(BF16),
             spread(jnp.exp2(late(bot_full - m_col)), 0.0).astype(BF16)], axis=0)
        ot = _dot(vt_aug[:, :(i + 1) * MOBA_BLOCK], p)
        denom = ot[LANES:LANES + 1]
        out_t = jnp.concatenate(
            [ot[h * HEAD_DIM:(h + 1) * HEAD_DIM, h * MOBA_BLOCK:(h + 1) * MOBA_BLOCK]
             / denom[:, h * MOBA_BLOCK:(h + 1) * MOBA_BLOCK] for h in range(heads)], axis=0)
        o_ref[i * MOBA_BLOCK:(i + 1) * MOBA_BLOCK, pr * LANES:(pr + 1) * LANES] = (
            out_t.T.astype(BF16))

    units = [(pr, i) for pr in range(len(pairs)) for i in range(nb)]
    st_next = scores(*units[0])
    for n, unit in enumerate(units):
        st = st_next
        if n + 1 < len(units):
            st_next = scores(*units[n + 1])
        finish(*unit, st)


def _attn(q2, k2, v2, bsz, seq, weights):
    width = ATTN_PAIRS * LANES
    pairs = ATTN_WIDTH // width
    n_steps = bsz * pairs
    spec = pl.BlockSpec((seq, width), lambda b, hp: (b, hp))

    def slice_spec(w):
        n = next(n for n in (n_steps, n_steps // 2, n_steps // 4, 1)
                 if w.shape[0] % n == 0 and (w.shape[0] // n) % BF16_ROWS == 0)
        return pl.BlockSpec((w.shape[0] // n, w.shape[1]),
                            lambda b, hp: (jnp.minimum(b * pairs + hp, n - 1), 0))

    w_specs = [slice_spec(w) for w in weights]
    out = pl.pallas_call(
        functools.partial(_attn_kernel, len(weights)),
        grid=(bsz, pairs),
        in_specs=[spec, spec, spec] + w_specs,
        out_specs=[spec] + w_specs,
        out_shape=[jax.ShapeDtypeStruct(q2.shape, BF16)]
        + [jax.ShapeDtypeStruct(w.shape, BF16) for w in weights],
        compiler_params=pltpu.CompilerParams(
            dimension_semantics=("arbitrary", "arbitrary"), vmem_limit_bytes=VMEM_LIMIT),
        name="attn",
    )(q2, k2, v2, *weights)
    return out[0], out[1:]


def _ordered_after(x, anchor):
    rows, width = anchor.shape
    parts = [anchor[r:r + BF16_ROWS, c:c + LANES]
             for r in range(0, rows, BF16_ROWS) for c in range(0, width, LANES)]
    while len(parts) > 1:
        parts = [jnp.maximum(parts[n], parts[n + 1]) for n in range(0, len(parts), 2)]
    a = parts[0]
    head = x[:BF16_ROWS, :LANES] + (a - a)
    top = jnp.concatenate([head, x[:BF16_ROWS, LANES:]], axis=1)
    return jnp.concatenate([top, x[BF16_ROWS:]], axis=0)


def _tail_kernel(tiles_per_seq, u0_ref, u_ref, halo_ref, x_ref, a_ref, p_ref, cw_ref, cb_ref, lg_ref,
                 lb_ref, wout_ref, gf_ref, wup_ref, wdn_ref, gp_ref, wg_ref, wp_ref, gfin_ref,
                 o_ref, conv_ref):
    i = pl.program_id(0)
    n_tiles = pl.num_programs(0)

    def conv_chunk_of(tile_ref, halo, c, not_before):
        def window(lanes):
            start = c * CONV_ROWS - CONV_PAD
            stop = (c + 1) * CONV_ROWS
            if start < 0:
                return jnp.concatenate([halo[CONV_PAD + start:, lanes], tile_ref[0:stop, lanes]],
                                       axis=0)
            return tile_ref[start:stop, lanes]
        y = _conv_rows(window, cw_ref, cb_ref, lg_ref, lb_ref, not_before)
        conv_ref[c * CONV_ROWS:(c + 1) * CONV_ROWS, :] = y
        return y

    @pl.when(i == 0)
    def _():
        zeros = jnp.zeros((CONV_PAD, CONV_CH), F32)
        start = u0_ref[0:BF16_ROWS, 0:LANES].astype(BF16)
        for c in range(u0_ref.shape[0] // CONV_ROWS):
            conv_chunk_of(u0_ref, zeros, c, start)

    tile = jnp.minimum(i + 1, n_tiles - 1)
    starts_seq = lax.rem(tile, tiles_per_seq) == 0
    halo = jnp.where(starts_seq, 0.0, halo_ref[...])
    conv_chunk = functools.partial(conv_chunk_of, u_ref, halo)

    h = (x_ref[...]
         + _dot(conv_ref[...], wout_ref[:CONV_CH, :])
         + _dot(a_ref[...], wout_ref[CONV_CH:, :]))
    hn = _rmsnorm(h, gf_ref[...]).astype(BF16)
    n_chunks = u_ref.shape[0] // CONV_ROWS
    bounds = [0, (4 * n_chunks) // 8, (7 * n_chunks) // 8, n_chunks]

    def conv_group(q, x, not_before):
        for c in range(bounds[q], bounds[q + 1]):
            y = conv_chunk(c, not_before)
            x = _ordered_after(x, y)
            not_before = y
        return x

    gt = _dot(hn, wup_ref[:, :D_FF])
    hn = conv_group(0, hn, hn)
    up = _dot(hn, wup_ref[:, D_FF:])
    act = conv_group(1, (jax.nn.silu(gt) * up).astype(BF16), hn)
    h = h + _dot(act, wdn_ref[...])
    hn = conv_group(2, _rmsnorm(h, gp_ref[...]).astype(BF16), act)
    gate = jax.nn.sigmoid(_dot(hn, wg_ref[...]))
    h = h + gate * _dot(p_ref[...].astype(BF16), wp_ref[...])
    o_ref[...] = _rmsnorm(h, gfin_ref[...])


def _tail(u2, x2, attn2, p2, seq, cw, cb, lg, lb, wout, gf, wup, wdn, gp, wg, wp, gfin):
    t = x2.shape[0]
    rows = TAIL_ROWS
    n_tiles = t // rows
    halo_per_tile = rows // CONV_PAD
    nxt = lambda i: jnp.minimum(i + 1, n_tiles - 1)
    row_spec = lambda width: pl.BlockSpec((rows, width), lambda i: (i, 0))
    full = lambda a: pl.BlockSpec(a.shape, lambda i: (0, 0), pipeline_mode=pl.Buffered(1))
    return pl.pallas_call(
        functools.partial(_tail_kernel, seq // rows),
        grid=(n_tiles,),
        in_specs=[pl.BlockSpec((rows, CONV_CH), lambda i: (0, 0)),
                  pl.BlockSpec((rows, CONV_CH), lambda i: (nxt(i), 0)),
                  pl.BlockSpec((CONV_PAD, CONV_CH), lambda i: (nxt(i) * halo_per_tile - 1, 0)),
                  row_spec(D_MODEL), row_spec(ATTN_WIDTH), row_spec(PLE_DIM),
                  full(cw), full(cb), full(lg), full(lb),
                  full(wout), full(gf), full(wup), full(wdn), full(gp), full(wg), full(wp),
                  full(gfin)],
        out_specs=row_spec(D_MODEL),
        out_shape=jax.ShapeDtypeStruct((t, D_MODEL), F32),
        scratch_shapes=[pltpu.VMEM((rows, CONV_CH), BF16)],
        compiler_params=pltpu.CompilerParams(
            dimension_semantics=("arbitrary",), vmem_limit_bytes=VMEM_LIMIT),
        name="tail",
    )(u2, u2, u2, x2, attn2, p2, cw, cb, lg, lb, wout, gf, wup, wdn, gp, wg, wp, gfin)


def kernel(x, p, positions, norm_mix_g, w_in, conv_w, conv_b, conv_ln_g, conv_ln_b, w_out,
           norm_ffn_g, w_ffn_up, w_ffn_down, norm_ple_g, w_ple_gate, w_ple_proj, final_norm_g):
    bsz, seq, _ = x.shape
    t = bsz * seq
    depth = w_in.shape[0]
    row = lambda a: a.reshape(1, -1)
    inv_freq = ROPE_THETA ** (-jnp.arange(0, ROPE_DIM, 2, dtype=F32) / ROPE_DIM)
    h = x.reshape(t, D_MODEL)
    for i in range(depth):
        u, q, k, v = _in_proj(h, positions, inv_freq, row(norm_mix_g[i]), w_in[i])
        attn, (wout, wup, wdn, wg, wp) = _attn(
            q, k, v, bsz, seq,
            [w_out[i], w_ffn_up[i], w_ffn_down[i], w_ple_gate[i], w_ple_proj[i]])
        assert depth == 1
        h = _tail(u, h, attn, p[i].reshape(t, PLE_DIM), seq, conv_w[i], row(conv_b[i]),
                  row(conv_ln_g[i]), row(conv_ln_b[i]), wout,
                  row(norm_ffn_g[i]), wup, wdn, row(norm_ple_g[i]), wg, wp,
                  row(final_norm_g))
    return h.reshape(bsz, seq, D_MODEL)
```
